```python
import math
import jax, jax.numpy as jnp
from jax import lax
import numpy as np

D_MODEL = 1024
BATCH = 4
SEQ = 8192
DEPTH = 4
DEC_BATCH = 16
DEC_SEQ = 4096
PAST_LEN = 128

N_MIXERS = 4
HEAD_DIM = 64
N_HEADS = 16
DILATIONS = ((128, 1), (512, 4), (2048, 16))
N_DIL = 3
B_KV_HEADS = 4
B_GROUP = N_HEADS // B_KV_HEADS
B_HALF_WINDOW = 128
NUM_BUCKETS = 32
MAX_DISTANCE = 1024
SHORT_CONV = 3
D_INNER = 2 * D_MODEL
SSM_HEAD_DIM = 64
SSM_HEADS = D_INNER // SSM_HEAD_DIM
SSM_GROUPS = 4
HEADS_PER_GROUP = SSM_HEADS // SSM_GROUPS
D_STATE = 128
SSM_CONV = 4
CHUNK = 128
CONV_DIM = D_INNER + 2 * SSM_GROUPS * D_STATE
D_FF = 2816
FFN_CONV = 3
EPS = 1e-6
NEG_INF = -1e30

kernel_name = 'hybrid_bidir_encoder_interleaved'


def rmsnorm(x, g):
    xf = x.astype(jnp.float32)
    y = xf * lax.rsqrt(jnp.mean(xf * xf, axis=-1, keepdims=True) + EPS) * g.astype(jnp.float32)
    return y.astype(x.dtype)


def dwconv(x, w):
    k, c = w.shape
    left = k // 2
    return lax.conv_general_dilated(x, w[:, None, :].astype(x.dtype), window_strides=(1,),
                                    padding=[(left, k - 1 - left)],
                                    dimension_numbers=('NWC', 'WIO', 'NWC'),
                                    feature_group_count=c)


def t5_bucket(rel):
    half = NUM_BUCKETS // 2
    exact = half // 2
    n = np.abs(rel)
    log_ratio = np.log(np.maximum(n, 1) / exact) / math.log(MAX_DISTANCE / exact)
    large = np.minimum(exact + (log_ratio * (half - exact)).astype(np.int64), half - 1)
    return np.where(rel > 0, half, 0) + np.where(n < exact, n, large)


def banded_attention(q, k, v, rel_bias, half, stride, sink=None):
    n, l, hk, g, dh = q.shape
    blk = half
    nb = -(-l // blk)
    pad = nb * blk - l
    qb = jnp.pad(q, ((0, 0), (0, pad), (0, 0), (0, 0), (0, 0))).reshape(n, nb, blk, hk, g, dh)

    def windows(t):
        tp = jnp.pad(t, ((0, 0), (blk, blk + pad), (0, 0), (0, 0))).reshape(n, nb + 2, blk, hk, dh)
        return jnp.concatenate([tp[:, :-2], tp[:, 1:-1], tp[:, 2:]], axis=2)

    kw, vw = windows(k), windows(v)
    rel = np.arange(3 * blk)[None, :] - blk - np.arange(blk)[:, None]
    kpos = np.arange(nb)[:, None] * blk + np.arange(3 * blk)[None, :] - blk
    mask = (np.abs(rel) <= half)[None] & ((kpos >= 0) & (kpos < l))[:, None, :]
    bias = rel_bias[t5_bucket(rel * stride)].astype(jnp.float32)
    bias = bias.reshape(blk, 3 * blk, hk, g).transpose(2, 3, 0, 1)
    s = jnp.einsum('nbqhgd,nbkhd->nbhgqk', qb, kw).astype(jnp.float32) * (HEAD_DIM ** -0.5) + bias
    s = jnp.where(mask[None, :, None, None], s, NEG_INF)
    m = s.max(axis=-1)
    if sink is not None:
        sk = sink.astype(jnp.float32).reshape(hk, g)[None, None, :, :, None]
        m = jnp.maximum(m, sk)
    p = jnp.exp(s - m[..., None])
    denom = p.sum(axis=-1)
    if sink is not None:
        denom = denom + jnp.exp(sk - m)
    o = jnp.einsum('nbhgqk,nbkhd->nbqhgd', p / denom[..., None], vw.astype(jnp.float32))
    lse = (m + jnp.log(denom)).transpose(0, 1, 4, 2, 3)
    o = o.reshape(n, nb * blk, hk, g, dh)[:, :l].astype(q.dtype)
    lse = lse.reshape(n, nb * blk, hk, g)[:, :l]
    return o, lse


def mixer_dilated(h, w_qkv, w_o, rel_bias):
    bsz, s, _ = h.shape
    qkv = (h @ w_qkv).reshape(bsz, s, N_DIL, 3, N_HEADS, HEAD_DIM)
    outs, lses = [], []
    for gi, (window, r) in enumerate(DILATIONS):
        def to_sub(t):
            return t.reshape(bsz, s // r, r, N_HEADS, HEAD_DIM).transpose(0, 2, 1, 3, 4).reshape(bsz * r, s // r, N_HEADS, HEAD_DIM)
        q, k, v = (to_sub(qkv[:, :, gi, j]) for j in range(3))
        o, lse = banded_attention(q[:, :, :, None], k, v, rel_bias, window // (2 * r), r)
        outs.append(o.reshape(bsz, r, s // r, N_HEADS, HEAD_DIM).transpose(0, 2, 1, 3, 4).reshape(bsz, s, N_HEADS, HEAD_DIM))
        lses.append(lse.reshape(bsz, r, s // r, N_HEADS).transpose(0, 2, 1, 3).reshape(bsz, s, N_HEADS))
    alpha = jax.nn.softmax(jnp.stack(lses), axis=0)
    o = jnp.einsum('zbsh,zbshd->bshd', alpha, jnp.stack(outs).astype(jnp.float32))
    return o.reshape(bsz, s, N_HEADS * HEAD_DIM).astype(h.dtype) @ w_o


def mixer_window(h, w_qkv, sink, w_o, rel_bias):
    bsz, s, _ = h.shape
    qkv = h @ w_qkv
    qw, kvw = N_HEADS * HEAD_DIM, B_KV_HEADS * HEAD_DIM
    q = qkv[..., :qw].reshape(bsz, s, B_KV_HEADS, B_GROUP, HEAD_DIM)
    k = qkv[..., qw:qw + kvw].reshape(bsz, s, B_KV_HEADS, HEAD_DIM)
    v = qkv[..., qw + kvw:].reshape(bsz, s, B_KV_HEADS, HEAD_DIM)
    o, _ = banded_attention(q, k, v, rel_bias, B_HALF_WINDOW, 1, sink)
    return o.reshape(bsz, s, qw) @ w_o


def mixer_shortconv(h, w_in, conv_w, w_out):
    bg, cg, xin = jnp.split(h @ w_in, 3, axis=-1)
    return (bg * dwconv(cg * xin, conv_w)) @ w_out


def ssd_chunked(x, adt, bm, cm):
    b, s, g, e, p = x.shape
    n = bm.shape[-1]
    nc = s // CHUNK
    x = x.reshape(b, nc, CHUNK, g, e, p)
    adt = adt.reshape(b, nc, CHUNK, g, e)
    bm = bm.reshape(b, nc, CHUNK, g, n).astype(jnp.float32)
    cm = cm.reshape(b, nc, CHUNK, g, n).astype(jnp.float32)
    a_cs = jnp.cumsum(adt, axis=2)
    a_t = jnp.moveaxis(a_cs, 2, -1)
    seg = a_t[..., :, None] - a_t[..., None, :]
    causal = np.tril(np.ones((CHUNK, CHUNK), dtype=bool))
    lmat = jnp.exp(jnp.where(causal, seg, -jnp.inf))
    cb = jnp.einsum('bclgn,bcsgn->bcgls', cm, bm)
    y_diag = jnp.einsum('bcgels,bcsgep->bclgep', cb[:, :, :, None] * lmat, x)
    decay_states = jnp.exp(a_cs[:, :, -1:] - a_cs)
    states = jnp.einsum('bclgn,bclge,bclgep->bcgepn', bm, decay_states, x)
    chunk_decay = jnp.exp(a_cs[:, :, -1])

    def step(hc, inp):
        st, dec = inp
        return hc * dec[..., None, None] + st, hc

    h0 = jnp.zeros((b, g, e, p, n), jnp.float32)
    _, states_in = lax.scan(step, h0, (jnp.moveaxis(states, 1, 0), jnp.moveaxis(chunk_decay, 1, 0)))
    states_in = jnp.moveaxis(states_in, 0, 1)
    y_off = jnp.einsum('bclgn,bcgepn,bclge->bclgep', cm, states_in, jnp.exp(a_cs))
    return (y_diag + y_off).reshape(b, s, g, e, p)


def mixer_ssd(h, w_in, conv_w, conv_b, dt_bias, a_log, d_skip, norm_g, w_out):
    bsz, s, _ = h.shape
    proj = h @ w_in
    z = proj[..., :D_INNER]
    xbc = proj[..., D_INNER:D_INNER + CONV_DIM]
    dt_raw = proj[..., D_INNER + CONV_DIM:].reshape(bsz, s, 2, SSM_HEADS)
    xbc = jax.nn.silu(dwconv(xbc, conv_w) + conv_b)
    gn = SSM_GROUPS * D_STATE
    xf = xbc[..., :D_INNER].reshape(bsz, s, SSM_GROUPS, HEADS_PER_GROUP, SSM_HEAD_DIM).astype(jnp.float32)
    bm = xbc[..., D_INNER:D_INNER + gn].reshape(bsz, s, SSM_GROUPS, D_STATE)
    cm = xbc[..., D_INNER + gn:].reshape(bsz, s, SSM_GROUPS, D_STATE)
    dt = jax.nn.softplus(dt_raw.astype(jnp.float32) + dt_bias.astype(jnp.float32))
    a = -jnp.exp(a_log.astype(jnp.float32))
    adt = (dt * a).reshape(bsz, s, 2, SSM_GROUPS, HEADS_PER_GROUP)
    dtg = dt.reshape(bsz, s, 2, SSM_GROUPS, HEADS_PER_GROUP)
    y_f = ssd_chunked(xf * dtg[:, :, 0, :, :, None], adt[:, :, 0], bm, cm)
    y_b = jnp.flip(ssd_chunked(jnp.flip(xf * dtg[:, :, 1, :, :, None], 1), jnp.flip(adt[:, :, 1], 1),
                               jnp.flip(bm, 1), jnp.flip(cm, 1)), 1)
    y = y_f + y_b + xf * d_skip.astype(jnp.float32).reshape(SSM_GROUPS, HEADS_PER_GROUP)[..., None]
    y = y.reshape(bsz, s, D_INNER) * jax.nn.silu(z.astype(jnp.float32))
    return rmsnorm(y, norm_g).astype(h.dtype) @ w_out


def conv_ffn(h, w_up, conv_w, conv_b, w_down):
    a, u = jnp.split(h @ w_up, 2, axis=-1)
    a = dwconv(a, conv_w) + conv_b
    return (jax.nn.silu(a) * u) @ w_down


def trunk(x, rel_bias, a_w_qkv, a_w_o, b_w_qkv, b_sink, b_w_o, c_w_in, c_conv_w, c_w_out,
          d_w_in, d_conv_w, d_conv_b, d_dt_bias, d_a_log, d_skip, d_norm_g, d_w_out,
          ffn_w_up, ffn_conv_w, ffn_conv_b, ffn_w_down, norm_g, final_g):
    for i in range(DEPTH):
        kind, j = i % N_MIXERS, i // N_MIXERS
        h = rmsnorm(x, norm_g[i, 0])
        if kind == 0:
            mix = mixer_dilated(h, a_w_qkv[j], a_w_o[j], rel_bias)
        elif kind == 1:
            mix = mixer_window(h, b_w_qkv[j], b_sink[j], b_w_o[j], rel_bias)
        elif kind == 2:
            mix = mixer_shortconv(h, c_w_in[j], c_conv_w[j], c_w_out[j])
        else:
            mix = mixer_ssd(h, d_w_in[j], d_conv_w[j], d_conv_b[j], d_dt_bias[j], d_a_log[j],
                            d_skip[j], d_norm_g[j], d_w_out[j])
        x = x + mix
        x = x + conv_ffn(rmsnorm(x, norm_g[i, 1]), ffn_w_up[i], ffn_conv_w[i], ffn_conv_b[i], ffn_w_down[i])
    return rmsnorm(x, final_g)


def setup_inputs(seed: int = 0) -> dict:
    key = jax.random.key(seed)
    ks = jax.random.split(key, 26)
    n_a, n_b, n_c, n_d = (len(range(t, DEPTH, N_MIXERS)) for t in range(N_MIXERS))

    def normal(k, shape):
        return jax.random.normal(k, shape, jnp.float32)

    def w(k, shape, fan_in):
        return normal(k, shape) * (fan_in ** -0.5)

    aw = N_HEADS * HEAD_DIM
    dt0 = jnp.exp(jax.random.uniform(ks[14], (n_d, 2, SSM_HEADS), jnp.float32,
                                     minval=math.log(1e-3), maxval=math.log(1e-1)))
    return {
        'x_prompt': normal(ks[0], (BATCH, SEQ, D_MODEL)),
        'x_sample': normal(ks[1], (DEC_BATCH, DEC_SEQ, D_MODEL)),
        'rel_bias': 0.2 * normal(ks[2], (NUM_BUCKETS, N_HEADS)),
        'a_w_qkv': w(ks[3], (n_a, D_MODEL, N_DIL * 3 * aw), D_MODEL),
        'a_w_o': w(ks[4], (n_a, aw, D_MODEL), aw),
        'b_w_qkv': w(ks[5], (n_b, D_MODEL, (N_HEADS + 2 * B_KV_HEADS) * HEAD_DIM), D_MODEL),
        'b_sink': 0.5 * normal(ks[6], (n_b, N_HEADS)),
        'b_w_o': w(ks[7], (n_b, aw, D_MODEL), aw),
        'c_w_in': w(ks[8], (n_c, D_MODEL, 3 * D_MODEL), D_MODEL),
        'c_conv_w': w(ks[9], (n_c, SHORT_CONV, D_MODEL), SHORT_CONV),
        'c_w_out': w(ks[10], (n_c, D_MODEL, D_MODEL), D_MODEL),
        'd_w_in': w(ks[11], (n_d, D_MODEL, D_INNER + CONV_DIM + 2 * SSM_HEADS), D_MODEL),
        'd_conv_w': w(ks[12], (n_d, SSM_CONV, CONV_DIM), SSM_CONV),
        'd_conv_b': 0.02 * normal(ks[13], (n_d, CONV_DIM)),
        'd_dt_bias': dt0 + jnp.log(-jnp.expm1(-dt0)),
        'd_a_log': jnp.log(jax.random.uniform(ks[15], (n_d, 2, SSM_HEADS), jnp.float32, minval=1.0, maxval=16.0)),
        'd_skip': 1.0 + 0.1 * normal(ks[16], (n_d, SSM_HEADS)),
        'd_norm_g': 1.0 + 0.05 * normal(ks[17], (n_d, D_INNER)),
        'd_w_out': w(ks[18], (n_d, D_INNER, D_MODEL), D_INNER),
        'ffn_w_up': w(ks[19], (DEPTH, D_MODEL, 2 * D_FF), D_MODEL),
        'ffn_conv_w': w(ks[20], (DEPTH, FFN_CONV, D_FF), FFN_CONV),
        'ffn_conv_b': 0.02 * normal(ks[21], (DEPTH, D_FF)),
        'ffn_w_down': w(ks[22], (DEPTH, D_FF, D_MODEL), D_FF),
        'norm_g': 1.0 + 0.05 * normal(ks[23], (DEPTH, 2, D_MODEL)),
        'final_g': 1.0 + 0.05 * normal(ks[24], (D_MODEL,)),
    }


def reference(x_prompt, x_sample, rel_bias, a_w_qkv, a_w_o, b_w_qkv, b_sink, b_w_o, c_w_in, c_conv_w, c_w_out,
              d_w_in, d_conv_w, d_conv_b, d_dt_bias, d_a_log, d_skip, d_norm_g, d_w_out,
              ffn_w_up, ffn_conv_w, ffn_conv_b, ffn_w_down, norm_g, final_g):
    y_prompt = trunk(x_prompt, rel_bias, a_w_qkv, a_w_o, b_w_qkv, b_sink, b_w_o, c_w_in, c_conv_w, c_w_out,
                     d_w_in, d_conv_w, d_conv_b, d_dt_bias, d_a_log, d_skip, d_norm_g, d_w_out,
                     ffn_w_up, ffn_conv_w, ffn_conv_b, ffn_w_down, norm_g, final_g)
    y_sample = trunk(x_sample, rel_bias, a_w_qkv, a_w_o, b_w_qkv, b_sink, b_w_o, c_w_in, c_conv_w, c_w_out,
                     d_w_in, d_conv_w, d_conv_b, d_dt_bias, d_a_log, d_skip, d_norm_g, d_w_out,
                     ffn_w_up, ffn_conv_w, ffn_conv_b, ffn_w_down, norm_g, final_g)
    return (y_prompt, y_sample)
```

```python
import functools
import math

import numpy as np
import jax
import jax.numpy as jnp
from jax import lax
from jax.experimental import pallas as pl
from jax.experimental.pallas import tpu as pltpu

F32 = jnp.float32
BF16 = jnp.bfloat16

D_MODEL = 1024
DEPTH = 4
HEAD_DIM = 64
N_HEADS = 16
DILATIONS = ((128, 1), (512, 4), (2048, 16))
N_DIL = 3
B_KV_HEADS = 4
B_HALF_WINDOW = 128
NUM_BUCKETS = 32
MAX_DISTANCE = 1024
D_INNER = 2 * D_MODEL
SSM_HEADS = 32
SSM_GROUPS = 4
D_STATE = 128
CHUNK = 128
CONV_DIM = D_INNER + 2 * SSM_GROUPS * D_STATE
D_FF = 2816
EPS = 1e-6
NEG_INF = -1e30

LANES = 128
BF16_ROWS = 16
VMEM_LIMIT = 56 * 1024 * 1024
ROW_TILE = 512
MID_CHUNK = 256
ATT_Q = 128


def _cparams(n_axes):
    return pltpu.CompilerParams(dimension_semantics=("arbitrary",) * n_axes,
                                vmem_limit_bytes=VMEM_LIMIT)


def _resident(shape):
    nd = len(shape)
    return pl.BlockSpec(shape, lambda *_: (0,) * nd, pipeline_mode=pl.Buffered(1))


def _rms(x, g):
    return x * lax.rsqrt(jnp.mean(x * x, axis=-1, keepdims=True) + EPS) * g


def _silu(t):
    return t / (1.0 + jnp.exp(-t))


def _norm_proj_kernel(x_ref, g_ref, w_ref, *rest, segs, col_chunk):
    o_refs, h_ref = rest[:-1], rest[-1]
    h_ref[...] = _rms(x_ref[...], g_ref[...]).astype(BF16)
    for o_ref, (start, width) in zip(o_refs, segs):
        for c in range(0, width, col_chunk):
            cw = min(col_chunk, width - c)
            o_ref[:, c:c + cw] = jnp.dot(h_ref[...], w_ref[:, start + c:start + c + cw],
                                         preferred_element_type=F32).astype(o_ref.dtype)


def _norm_proj(x2, g, w, segs, dtypes, tm=ROW_TILE):
    m, d = x2.shape
    n = w.shape[1]
    out_shape = [jax.ShapeDtypeStruct((m, wd), dt) for (_, wd), dt in zip(segs, dtypes)]
    return pl.pallas_call(
        functools.partial(_norm_proj_kernel, segs=tuple(segs), col_chunk=512),
        grid=(m // tm,),
        in_specs=[pl.BlockSpec((tm, d), lambda i: (i, 0)),
                  pl.BlockSpec((1, d), lambda i: (0, 0)),
                  _resident((d, n))],
        out_specs=[pl.BlockSpec((tm, wd), lambda i: (i, 0)) for (_, wd) in segs],
        out_shape=out_shape,
        scratch_shapes=[pltpu.VMEM((tm, d), BF16)],
        compiler_params=_cparams(1),
        name="norm_proj",
    )(x2, g.reshape(1, d), w)


HALO = BF16_ROWS


def _conv3(t, w, tm):
    n = tm + 2 * HALO
    up = pltpu.roll(t, 1, 0)[HALO:HALO + tm]
    dn = pltpu.roll(t, n - 1, 0)[HALO:HALO + tm]
    return w[0:1] * up + w[1:2] * t[HALO:HALO + tm] + w[2:3] * dn


def _gated_mlp_kernel(xp_ref, x_ref, xn_ref, g_ref, *rest, mode, tm, tiles_per_seq, nchunks,
                      final_norm):
    if mode == "ffn":
        wa_ref, wu_ref, cw_ref, cb_ref, wd_ref, fg_ref, o_ref, h_ref, acc_ref = rest
    else:
        wb_ref, wc_ref, wx_ref, cw_ref, wd_ref, fg_ref, o_ref, h_ref, acc_ref = rest
    pos = pl.program_id(0) % tiles_per_seq
    fprev = (pos != 0).astype(F32)
    fnext = (pos != tiles_per_seq - 1).astype(F32)
    g = g_ref[...]
    h_ref[0:HALO, :] = (_rms(xp_ref[...], g) * fprev).astype(BF16)
    h_ref[HALO:HALO + tm, :] = _rms(x_ref[...], g).astype(BF16)
    h_ref[HALO + tm:, :] = (_rms(xn_ref[...], g) * fnext).astype(BF16)
    acc_ref[...] = jnp.zeros_like(acc_ref)

    def body(c, carry):
        h_ext = h_ref[...]
        h_main = h_ref[HALO:HALO + tm, :]
        if mode == "ffn":
            a = jnp.dot(h_ext, wa_ref[c], preferred_element_type=F32)
            u = jnp.dot(h_main, wu_ref[c], preferred_element_type=F32)
            t = _conv3(a, cw_ref[c], tm) + cb_ref[c]
            gt = _silu(t) * u
        else:
            cg = jnp.dot(h_ext, wc_ref[c], preferred_element_type=F32)
            xi = jnp.dot(h_ext, wx_ref[c], preferred_element_type=F32)
            bg = jnp.dot(h_main, wb_ref[c], preferred_element_type=F32)
            gt = bg * _conv3(cg * xi, cw_ref[c], tm)
        acc_ref[...] += jnp.dot(gt.astype(BF16), wd_ref[c], preferred_element_type=F32)
        return carry

    lax.fori_loop(0, nchunks, body, 0)
    y = x_ref[...] + acc_ref[...]
    if final_norm:
        y = _rms(y, fg_ref[...])
    o_ref[...] = y


def _chunk_cols(w, cw=MID_CHUNK):
    k, n = w.shape
    return w.reshape(k, n // cw, cw).transpose(1, 0, 2)


def _gated_mlp(x2, g, weights, mode, seq, final_g=None, tm=ROW_TILE):
    m, d = x2.shape
    nchunks = weights[-1].shape[0]
    hb = tm // HALO
    n_hblocks = m // HALO
    in_specs = [
        pl.BlockSpec((HALO, d), lambda i: (jnp.maximum(i * hb - 1, 0), 0)),
        pl.BlockSpec((tm, d), lambda i: (i, 0)),
        pl.BlockSpec((HALO, d), lambda i: (jnp.minimum((i + 1) * hb, n_hblocks - 1), 0)),
        pl.BlockSpec((1, d), lambda i: (0, 0)),
    ] + [_resident(w.shape) for w in weights] + [pl.BlockSpec((1, d), lambda i: (0, 0))]
    fg = (final_g if final_g is not None else jnp.ones((d,), F32)).reshape(1, d)
    return pl.pallas_call(
        functools.partial(_gated_mlp_kernel, mode=mode, tm=tm, tiles_per_seq=seq // tm,
                          nchunks=nchunks, final_norm=final_g is not None),
        grid=(m // tm,),
        in_specs=in_specs,
        out_specs=pl.BlockSpec((tm, d), lambda i: (i, 0)),
        out_shape=jax.ShapeDtypeStruct((m, d), F32),
        scratch_shapes=[pltpu.VMEM((tm + 2 * HALO, d), BF16), pltpu.VMEM((tm, d), F32)],
        compiler_params=_cparams(1),
        name="gated_mlp_" + mode,
    )(x2, x2, x2, g.reshape(1, d), *weights, fg)


def _out_proj_kernel(x_ref, o_ref, w_ref, y_ref):
    y_ref[...] = x_ref[...] + jnp.dot(o_ref[...].astype(BF16), w_ref[...],
                                      preferred_element_type=F32)


def _out_proj(x2, o2, w, tm=ROW_TILE):
    m, d = x2.shape
    k = o2.shape[1]
    return pl.pallas_call(
        _out_proj_kernel,
        grid=(m // tm,),
        in_specs=[pl.BlockSpec((tm, d), lambda i: (i, 0)),
                  pl.BlockSpec((tm, k), lambda i: (i, 0)),
                  _resident((k, d))],
        out_specs=pl.BlockSpec((tm, d), lambda i: (i, 0)),
        out_shape=jax.ShapeDtypeStruct((m, d), F32),
        compiler_params=_cparams(1),
        name="out_proj",
    )(x2, o2, w)


def _ssd_out_kernel(x_ref, y_ref, z_ref, g_ref, w_ref, o_ref):
    yv = y_ref[...] * _silu(z_ref[...].astype(F32))
    hn = _rms(yv, g_ref[...]).astype(BF16)
    o_ref[...] = x_ref[...] + jnp.dot(hn, w_ref[...], preferred_element_type=F32)


def _ssd_out(x2, y2, z2, g, w, tm=ROW_TILE):
    m, d = x2.shape
    k = y2.shape[1]
    return pl.pallas_call(
        _ssd_out_kernel,
        grid=(m // tm,),
        in_specs=[pl.BlockSpec((tm, d), lambda i: (i, 0)),
                  pl.BlockSpec((tm, k), lambda i: (i, 0)),
                  pl.BlockSpec((tm, k), lambda i: (i, 0)),
                  pl.BlockSpec((1, k), lambda i: (0, 0)),
                  _resident((k, d))],
        out_specs=pl.BlockSpec((tm, d), lambda i: (i, 0)),
        out_shape=jax.ShapeDtypeStruct((m, d), F32),
        compiler_params=_cparams(1),
        name="ssd_out",
    )(x2, y2, z2, g.reshape(1, k), w)


def _attn_kernel(*refs, q_rows, halo, pair_heads, kv_div, has_sink, has_prev, want_lse, nb):
    refs = list(refs)
    q_ref, kp_ref, k_ref, kn_ref, vp_ref, v_ref, vn_ref, bias_ref = refs[:8]
    refs = refs[8:]
    sink_ref = refs.pop(0) if has_sink else None
    if has_prev:
        op_ref, lp_ref = refs.pop(0), refs.pop(0)
    o_ref = refs.pop(0)
    lse_ref = refs.pop(0) if want_lse else None
    kcat, vcat = refs
    q, h = q_rows, halo
    i = pl.program_id(2)
    variant = jnp.where(i == 0, 0, jnp.where(i == nb - 1, 2, 1))
    kcat[0:h] = kp_ref[0]
    kcat[h:h + q] = k_ref[0]
    kcat[h + q:] = kn_ref[0]
    vcat[0:h] = vp_ref[0]
    vcat[h:h + q] = v_ref[0]
    vcat[h + q:] = vn_ref[0]
    lane = lax.broadcasted_iota(jnp.int32, (q, LANES), 1)
    lo = lane < HEAD_DIM
    lo_b = (lax.broadcasted_iota(jnp.int32, (1, LANES), 1) < HEAD_DIM).astype(BF16)
    hi_b = 1.0 - lo_b
    lse_tile = jnp.zeros((q, LANES), F32)
    for p, heads in enumerate(pair_heads):
        cols = slice(p * LANES, (p + 1) * LANES)
        kcols = slice((p // kv_div) * LANES, (p // kv_div + 1) * LANES)
        q2 = q_ref[0, :, cols]
        qs = jnp.concatenate([q2 * lo_b, q2 * hi_b], axis=0)
        s = lax.dot_general(qs, kcat[:, kcols], (((1,), (1,)), ((), ())),
                            preferred_element_type=F32) + bias_ref[variant, p]
        m = jnp.max(s, axis=-1, keepdims=True)
        if has_sink:
            sk = jnp.concatenate([jnp.full((q, 1), sink_ref[heads[0]], F32),
                                  jnp.full((q, 1), sink_ref[heads[1]], F32)], axis=0)
            m = jnp.maximum(m, sk)
        e = jnp.exp(s - m)
        l = jnp.sum(e, axis=-1, keepdims=True)
        if has_sink:
            l = l + jnp.exp(sk - m)
        pv = jnp.dot(e.astype(BF16), vcat[:, kcols], preferred_element_type=F32)
        inv = 1.0 / l
        w0, w1 = inv[:q], inv[q:]
        if want_lse or has_prev:
            lse = m + jnp.log(l)
            l0, l1 = lse[:q], lse[q:]
        if has_prev:
            lp0 = lp_ref[0, :, 2 * p:2 * p + 1]
            lp1 = lp_ref[0, :, 2 * p + 1:2 * p + 2]
            n0 = jnp.maximum(lp0, l0)
            n0 = n0 + jnp.log(jnp.exp(lp0 - n0) + jnp.exp(l0 - n0))
            n1 = jnp.maximum(lp1, l1)
            n1 = n1 + jnp.log(jnp.exp(lp1 - n1) + jnp.exp(l1 - n1))
            w0 = w0 * jnp.exp(l0 - n0)
            w1 = w1 * jnp.exp(l1 - n1)
            wprev = jnp.where(lo, jnp.exp(lp0 - n0), jnp.exp(lp1 - n1))
            l0, l1 = n0, n1
        o = jnp.where(lo, pv[:q] * w0, pv[q:] * w1)
        if has_prev:
            o = o + op_ref[0, :, cols] * wprev
        o_ref[0, :, cols] = o.astype(o_ref.dtype)
        if want_lse:
            lse_tile = jnp.where(lane == 2 * p, l0, jnp.where(lane == 2 * p + 1, l1, lse_tile))
    if want_lse:
        lse_ref[0] = lse_tile


def _attention(q_arr, k_arr, v_arr, qcol, kcol, vcol, kv_width, bias, length, stride, halo,
               pair_heads, kv_div, sink=None, prev=None, want_lse=False, out_dtype=F32):
    b = q_arr.shape[0]
    q = ATT_Q
    nb = length // q
    assert nb >= 2 and length % q == 0 and q % halo == 0
    qh = q // halo
    n_hblocks = length // halo
    nk = q + 2 * halo
    width = N_HEADS * HEAD_DIM

    def main_map(col):
        return lambda bi, c, i: (bi, i, col(c))

    def prev_map(col):
        return lambda bi, c, i: (bi, jnp.maximum(i * qh - 1, 0), col(c))

    def next_map(col):
        return lambda bi, c, i: (bi, jnp.minimum((i + 1) * qh, n_hblocks - 1), col(c))

    in_specs = [pl.BlockSpec((1, q, width), main_map(qcol))]
    args = [q_arr]
    for arr, col in ((k_arr, kcol), (v_arr, vcol)):
        in_specs += [pl.BlockSpec((1, halo, kv_width), prev_map(col)),
                     pl.BlockSpec((1, q, kv_width), main_map(col)),
                     pl.BlockSpec((1, halo, kv_width), next_map(col))]
        args += [arr, arr, arr]
    in_specs.append(_resident(bias.shape))
    args.append(bias)
    if sink is not None:
        in_specs.append(pl.BlockSpec(memory_space=pltpu.SMEM))
        args.append(sink)
    if prev is not None:
        in_specs += [pl.BlockSpec((1, q, width), main_map(lambda c: c)),
                     pl.BlockSpec((1, q, LANES), main_map(lambda c: c))]
        args += list(prev)
    out_specs = [pl.BlockSpec((1, q, width), main_map(lambda c: c))]
    out_shape = [jax.ShapeDtypeStruct((b, length, stride * width), out_dtype)]
    if want_lse:
        out_specs.append(pl.BlockSpec((1, q, LANES), main_map(lambda c: c)))
        out_shape.append(jax.ShapeDtypeStruct((b, length, stride * LANES), F32))
    return pl.pallas_call(
        functools.partial(_attn_kernel, q_rows=q, halo=halo, pair_heads=tuple(pair_heads),
                          kv_div=kv_div, has_sink=sink is not None, has_prev=prev is not None,
                          want_lse=want_lse, nb=nb),
        grid=(b, stride, nb),
        in_specs=in_specs,
        out_specs=out_specs,
        out_shape=out_shape,
        scratch_shapes=[pltpu.VMEM((nk, kv_width), BF16), pltpu.VMEM((nk, kv_width), BF16)],
        compiler_params=_cparams(3),
        name="banded_attention",
    )(*args)


def _t5_bucket(rel):
    half = NUM_BUCKETS // 2
    exact = half // 2
    n = np.abs(rel)
    log_ratio = np.log(np.maximum(n, 1) / exact) / math.log(MAX_DISTANCE / exact)
    large = np.minimum(exact + (log_ratio * (half - exact)).astype(np.int64), half - 1)
    return np.where(rel > 0, half, 0) + np.where(n < exact, n, large)


def _attn_bias(rel_bias, halo, stride, pair_heads):
    q = ATT_Q
    nk = q + 2 * halo
    i = np.arange(q)[:, None]
    j = np.arange(nk)[None, :]
    rel = (j - halo) - i
    band = np.abs(rel) <= halo
    table = rel_bias.astype(F32)[_t5_bucket(rel * stride)]
    heads = np.asarray(pair_heads).reshape(-1)
    table = jnp.transpose(table, (2, 0, 1))[heads].reshape(len(pair_heads), 2 * q, nk)
    valid = np.stack([band & (j >= halo), band, band & (j < halo + q)])
    valid = np.concatenate([valid, valid], axis=1)[:, None]
    return jnp.where(valid, table[None], NEG_INF)


SSD_HALO = BF16_ROWS
SSD_EXT = CHUNK + 2 * SSD_HALO


def _ssd_kernel(xp_ref, x_ref, xn_ref, dt_ref, cw_ref, cb_ref, dtb_ref, a_ref, dsk_ref, *rest,
                direction, nc, has_yin):
    if has_yin:
        yin_ref, y_ref, xe_ref, xc_ref, st_ref = rest
    else:
        y_ref, xe_ref, xc_ref, st_ref = rest
    d = direction
    c = pl.program_id(1)
    cc = c if d == 0 else nc - 1 - c
    fprev = (cc != 0).astype(F32)
    fnext = (cc != nc - 1).astype(F32)

    @pl.when(c == 0)
    def _():
        st_ref[...] = jnp.zeros_like(st_ref)

    h0 = SSD_HALO
    xe_ref[0:h0] = xp_ref[0].astype(F32) * fprev
    xe_ref[h0:h0 + CHUNK] = x_ref[0].astype(F32)
    xe_ref[h0 + CHUNK:] = xn_ref[0].astype(F32) * fnext
    cchunk = 512
    for j in range(0, CONV_DIM, cchunk):
        xe = xe_ref[:, j:j + cchunk]
        w = cw_ref[:, j:j + cchunk]
        t = (w[0:1] * pltpu.roll(xe, 2, 0)[h0:h0 + CHUNK]
             + w[1:2] * pltpu.roll(xe, 1, 0)[h0:h0 + CHUNK]
             + w[2:3] * xe[h0:h0 + CHUNK]
             + w[3:4] * pltpu.roll(xe, SSD_EXT - 1, 0)[h0:h0 + CHUNK]) + cb_ref[:, j:j + cchunk]
        xc_ref[:, j:j + cchunk] = _silu(t)

    dtr = dt_ref[0] + dtb_ref[...]
    dt = jnp.maximum(dtr, 0.0) + jnp.log1p(jnp.exp(-jnp.abs(dtr)))
    adt = dt * a_ref[...]
    dt_t = dt.T
    adt_t = adt.T
    li = lax.broadcasted_iota(jnp.int32, (CHUNK, CHUNK), 0)
    ui = lax.broadcasted_iota(jnp.int32, (CHUNK, CHUNK), 1)
    mask = (ui <= li) if d == 0 else (ui >= li)
    tri = mask.astype(F32)
    hp = lax.Precision.HIGHEST
    acs = jnp.dot(tri, adt, precision=hp, preferred_element_type=F32)
    acs_t = lax.dot_general(adt_t, tri, (((1,), (1,)), ((), ())), precision=hp,
                            preferred_element_type=F32)
    tot = jnp.sum(adt_t, axis=1, keepdims=True)
    lo = lax.broadcasted_iota(jnp.int32, (CHUNK, LANES), 1) < HEAD_DIM

    heads_per_group = SSM_HEADS // SSM_GROUPS
    pairs_per_group = heads_per_group // 2
    for grp in range(SSM_GROUPS):
        bcol = D_INNER + grp * D_STATE
        ccol = D_INNER + SSM_GROUPS * D_STATE + grp * D_STATE
        bg = xc_ref[:, bcol:bcol + D_STATE]
        cg = xc_ref[:, ccol:ccol + D_STATE].astype(BF16)
        cb = lax.dot_general(cg, bg.astype(BF16), (((1,), (1,)), ((), ())),
                             preferred_element_type=F32)
        bg_t = bg.T
        for pp in range(pairs_per_group):
            p = grp * pairs_per_group + pp
            cols = slice(p * LANES, (p + 1) * LANES)
            x2 = xc_ref[:, cols]
            x2b = x2.astype(BF16)
            zero = jnp.zeros_like(x2b)
            xbd = jnp.concatenate([jnp.where(lo, x2b, zero), jnp.where(lo, zero, x2b)], axis=0)
            ms, bts, es, cds = [], [], [], []
            for hh in (d * SSM_HEADS + 2 * p, d * SSM_HEADS + 2 * p + 1):
                a_col = jnp.broadcast_to(acs[:, hh:hh + 1], (CHUNK, CHUNK))
                a_row = acs_t[hh:hh + 1, :]
                dt_row = dt_t[hh:hh + 1, :]
                lmat = jnp.exp(jnp.where(mask, a_col - a_row, NEG_INF))
                ms.append((cb * lmat * dt_row).astype(BF16))
                w_row = jnp.exp(tot[hh:hh + 1, :] - a_row) * dt_row
                bts.append((bg_t * w_row).astype(BF16))
                es.append(jnp.exp(a_col))
                cds.append(jnp.exp(tot[hh:hh + 1, :]))
            st = st_ref[p]
            y = jnp.dot(jnp.concatenate(ms, axis=1), xbd, preferred_element_type=F32)
            y = y + jnp.dot(cg, st.astype(BF16), preferred_element_type=F32) * jnp.where(lo, es[0], es[1])
            if d == 0:
                y = y + x2 * dsk_ref[:, cols]
            if has_yin:
                y = y + yin_ref[0, :, cols]
            y_ref[0, :, cols] = y
            st_ref[p] = (st * jnp.where(lo, cds[0], cds[1])
                         + jnp.dot(jnp.concatenate(bts, axis=1), xbd, preferred_element_type=F32))


def _ssd(xbc, dt_raw, conv_w, conv_b, dt_bias, a_neg, d_skip, direction, y_in=None):
    b, s, _ = xbc.shape
    nc = s // CHUNK
    hb = CHUNK // SSD_HALO
    n_hblocks = s // SSD_HALO

    def cc(c):
        return c if direction == 0 else nc - 1 - c

    in_specs = [
        pl.BlockSpec((1, SSD_HALO, CONV_DIM), lambda bi, c: (bi, jnp.maximum(cc(c) * hb - 1, 0), 0)),
        pl.BlockSpec((1, CHUNK, CONV_DIM), lambda bi, c: (bi, cc(c), 0)),
        pl.BlockSpec((1, SSD_HALO, CONV_DIM),
                     lambda bi, c: (bi, jnp.minimum((cc(c) + 1) * hb, n_hblocks - 1), 0)),
        pl.BlockSpec((1, CHUNK, LANES), lambda bi, c: (bi, cc(c), 0)),
        pl.BlockSpec(conv_w.shape, lambda bi, c: (0, 0)),
        pl.BlockSpec((1, CONV_DIM), lambda bi, c: (0, 0)),
        pl.BlockSpec((1, LANES), lambda bi, c: (0, 0)),
        pl.BlockSpec((1, LANES), lambda bi, c: (0, 0)),
        pl.BlockSpec((1, D_INNER), lambda bi, c: (0, 0)),
    ]
    args = [xbc, xbc, xbc, dt_raw, conv_w, conv_b.reshape(1, CONV_DIM), dt_bias, a_neg, d_skip]
    if y_in is not None:
        in_specs.append(pl.BlockSpec((1, CHUNK, D_INNER), lambda bi, c: (bi, cc(c), 0)))
        args.append(y_in)
    return pl.pallas_call(
        functools.partial(_ssd_kernel, direction=direction, nc=nc, has_yin=y_in is not None),
        grid=(b, nc),
        in_specs=in_specs,
        out_specs=pl.BlockSpec((1, CHUNK, D_INNER), lambda bi, c: (bi, cc(c), 0)),
        out_shape=jax.ShapeDtypeStruct((b, s, D_INNER), F32),
        scratch_shapes=[pltpu.VMEM((SSD_EXT, CONV_DIM), F32),
                        pltpu.VMEM((CHUNK, CONV_DIM), F32),
                        pltpu.VMEM((SSM_HEADS // 2, D_STATE, LANES), F32)],
        compiler_params=_cparams(2),
        name="ssd_scan",
    )(*args)


A_PAIR_HEADS = tuple((2 * p, 2 * p + 1) for p in range(N_HEADS // 2))
B_PAIR_HEADS = tuple((8 * t + j, 8 * t + 4 + j) for t in range(2) for j in range(4))


def _prepare(p):
    aw = N_HEADS * HEAD_DIM
    scale = HEAD_DIM ** -0.5
    out = {}
    wa = p["a_w_qkv"][0].reshape(D_MODEL, N_DIL, 3, aw)
    wa = wa * jnp.asarray([scale, 1.0, 1.0], F32)[None, None, :, None]
    out["a_w_qkv"] = wa.reshape(D_MODEL, N_DIL * 3 * aw).astype(BF16)
    out["a_w_o"] = p["a_w_o"][0].astype(BF16)
    perm = np.asarray(B_PAIR_HEADS).reshape(-1)
    cols = (perm[:, None] * HEAD_DIM + np.arange(HEAD_DIM)[None, :]).reshape(-1)
    wb = p["b_w_qkv"][0]
    out["b_w_qkv"] = jnp.concatenate([wb[:, :aw][:, cols] * scale, wb[:, aw:]], axis=1).astype(BF16)
    out["b_w_o"] = p["b_w_o"][0][cols, :].astype(BF16)
    out["b_sink"] = p["b_sink"][0].astype(F32)
    wc = p["c_w_in"][0].astype(BF16)
    out["c_w"] = (_chunk_cols(wc[:, :D_MODEL]), _chunk_cols(wc[:, D_MODEL:2 * D_MODEL]),
                  _chunk_cols(wc[:, 2 * D_MODEL:]), _chunk_cols(p["c_conv_w"][0].astype(F32)),
                  p["c_w_out"][0].astype(BF16).reshape(D_MODEL // MID_CHUNK, MID_CHUNK, D_MODEL))
    wd = p["d_w_in"][0]
    pad = jnp.zeros((D_MODEL, LANES - 2 * SSM_HEADS), F32)
    out["d_w_in"] = jnp.concatenate([wd, pad], axis=1).astype(BF16)
    lane_pad = jnp.zeros((LANES - 2 * SSM_HEADS,), F32)
    out["d_conv_w"] = p["d_conv_w"][0].astype(F32)
    out["d_conv_b"] = p["d_conv_b"][0].astype(F32)
    out["d_dt_bias"] = jnp.concatenate([p["d_dt_bias"][0].reshape(-1).astype(F32), lane_pad]).reshape(1, LANES)
    out["d_a_neg"] = jnp.concatenate([-jnp.exp(p["d_a_log"][0].reshape(-1).astype(F32)), lane_pad]).reshape(1, LANES)
    out["d_skip"] = jnp.repeat(p["d_skip"][0].astype(F32), HEAD_DIM).reshape(1, D_INNER)
    out["d_norm_g"] = p["d_norm_g"][0].astype(F32)
    out["d_w_out"] = p["d_w_out"][0].astype(BF16)
    ffn = []
    for i in range(DEPTH):
        wu = p["ffn_w_up"][i].astype(BF16)
        ffn.append((_chunk_cols(wu[:, :D_FF]), _chunk_cols(wu[:, D_FF:]),
                    _chunk_cols(p["ffn_conv_w"][i].astype(F32)),
                    _chunk_cols(p["ffn_conv_b"][i].astype(F32).reshape(1, D_FF)),
                    p["ffn_w_down"][i].astype(BF16).reshape(D_FF // MID_CHUNK, MID_CHUNK, D_MODEL)))
    out["ffn"] = ffn
    out["a_bias"] = [_attn_bias(p["rel_bias"], window // (2 * r), r, A_PAIR_HEADS)
                     for window, r in DILATIONS]
    out["b_bias"] = _attn_bias(p["rel_bias"], B_HALF_WINDOW, 1, B_PAIR_HEADS)
    out["norm_g"] = p["norm_g"].astype(F32)
    out["final_g"] = p["final_g"].astype(F32)
    return out


def _mixer_dilated(x2, g, w, bsz, seq):
    aw = N_HEADS * HEAD_DIM
    n = N_DIL * 3 * aw
    (qkv,) = _norm_proj(x2, g, w["a_w_qkv"], [(0, n)], [BF16])
    prev = None
    for gi, (window, r) in enumerate(DILATIONS):
        length = seq // r
        view = qkv.reshape(bsz, length, r * n)
        nblk = n // aw
        last = gi == N_DIL - 1
        res = _attention(
            view, view, view,
            qcol=lambda c, gi=gi: c * nblk + gi * 3,
            kcol=lambda c, gi=gi: c * nblk + gi * 3 + 1,
            vcol=lambda c, gi=gi: c * nblk + gi * 3 + 2,
            kv_width=aw, bias=w["a_bias"][gi], length=length, stride=r, halo=window // (2 * r),
            pair_heads=A_PAIR_HEADS, kv_div=1, prev=prev, want_lse=not last, out_dtype=F32)
        if last:
            o = res[0]
        else:
            nxt = DILATIONS[gi + 1][1]
            prev = (res[0].reshape(bsz, seq // nxt, nxt * aw), res[1].reshape(bsz, seq // nxt, nxt * LANES))
    return _out_proj(x2, o.reshape(bsz * seq, aw), w["a_w_o"])


def _mixer_window(x2, g, w, bsz, seq):
    aw = N_HEADS * HEAD_DIM
    kvw = B_KV_HEADS * HEAD_DIM
    n = aw + 2 * kvw
    (qkv,) = _norm_proj(x2, g, w["b_w_qkv"], [(0, n)], [BF16])
    view = qkv.reshape(bsz, seq, n)
    (o,) = _attention(view[:, :, :aw], view, view,
                      qcol=lambda c: 0, kcol=lambda c: aw // kvw, vcol=lambda c: aw // kvw + 1,
                      kv_width=kvw, bias=w["b_bias"], length=seq, stride=1, halo=B_HALF_WINDOW,
                      pair_heads=B_PAIR_HEADS, kv_div=4, sink=w["b_sink"], out_dtype=BF16)
    return _out_proj(x2, o.reshape(bsz * seq, aw), w["b_w_o"])


def _mixer_ssd(x2, g, w, bsz, seq):
    z, xbc, dt_raw = _norm_proj(
        x2, g, w["d_w_in"],
        [(0, D_INNER), (D_INNER, CONV_DIM), (D_INNER + CONV_DIM, LANES)], [BF16, BF16, F32])
    xbc = xbc.reshape(bsz, seq, CONV_DIM)
    dt_raw = dt_raw.reshape(bsz, seq, LANES)
    common = (w["d_conv_w"], w["d_conv_b"], w["d_dt_bias"], w["d_a_neg"], w["d_skip"])
    y = _ssd(xbc, dt_raw, *common, direction=0)
    y = _ssd(xbc, dt_raw, *common, direction=1, y_in=y)
    return _ssd_out(x2, y.reshape(bsz * seq, D_INNER), z, w["d_norm_g"], w["d_w_out"])


def _trunk(x, w):
    bsz, seq, d = x.shape
    x2 = x.reshape(bsz * seq, d)
    ng = w["norm_g"]
    for i in range(DEPTH):
        kind = i % 4
        if kind == 0:
            x2 = _mixer_dilated(x2, ng[i, 0], w, bsz, seq)
        elif kind == 1:
            x2 = _mixer_window(x2, ng[i, 0], w, bsz, seq)
        elif kind == 2:
            x2 = _gated_mlp(x2, ng[i, 0], w["c_w"], "sconv", seq)
        else:
            x2 = _mixer_ssd(x2, ng[i, 0], w, bsz, seq)
        x2 = _gated_mlp(x2, ng[i, 1], w["ffn"][i], "ffn", seq,
                        final_g=w["final_g"] if i == DEPTH - 1 else None)
    return x2.reshape(bsz, seq, d)


def kernel(x_prompt, x_sample, rel_bias, a_w_qkv, a_w_o, b_w_qkv, b_sink, b_w_o, c_w_in, c_conv_w,
           c_w_out, d_w_in, d_conv_w, d_conv_b, d_dt_bias, d_a_log, d_skip, d_norm_g, d_w_out,
           ffn_w_up, ffn_conv_w, ffn_conv_b, ffn_w_down, norm_g, final_g):
    w = _prepare(dict(
        rel_bias=rel_bias, a_w_qkv=a_w_qkv, a_w_o=a_w_o, b_w_qkv=b_w_qkv, b_sink=b_sink, b_w_o=b_w_o,
        c_w_in=c_w_in, c_conv_w=c_conv_w, c_w_out=c_w_out, d_w_in=d_w_in, d_conv_w=d_conv_w,
        d_conv_b=d_conv_b, d_dt_bias=d_dt_bias, d_a_log=d_a_log, d_skip=d_skip, d_norm_g=d_norm_g,
        d_w_out=d_w_out, ffn_w_up=ffn_w_up, ffn_conv_w=ffn_conv_w, ffn_conv_b=ffn_conv_b,
        ffn_w_down=ffn_w_down, norm_g=norm_g, final_g=final_g))
    return (_trunk(x_prompt, w), _trunk(x_sample, w))
```

```python
import functools
import math

import numpy as np
import jax
import jax.numpy as jnp
from jax import lax
from jax.experimental import pallas as pl
from jax.experimental.pallas import tpu as pltpu

F32 = jnp.float32
BF16 = jnp.bfloat16

D_MODEL = 1024
DEPTH = 4
HEAD_DIM = 64
N_HEADS = 16
DILATIONS = ((128, 1), (512, 4), (2048, 16))
N_DIL = 3
B_KV_HEADS = 4
B_HALF_WINDOW = 128
NUM_BUCKETS = 32
MAX_DISTANCE = 1024
D_INNER = 2 * D_MODEL
SSM_HEADS = 32
SSM_GROUPS = 4
D_STATE = 128
CHUNK = 128
CONV_DIM = D_INNER + 2 * SSM_GROUPS * D_STATE
D_FF = 2816
EPS = 1e-6
NEG_INF = -1e30

LANES = 128
BF16_ROWS = 16
VMEM_LIMIT = 56 * 1024 * 1024
ROW_TILE = 512
MID_CHUNK = 256
ATT_Q = 128


def _cparams(n_axes):
    return pltpu.CompilerParams(dimension_semantics=("arbitrary",) * n_axes,
                                vmem_limit_bytes=VMEM_LIMIT)


def _resident(shape):
    nd = len(shape)
    return pl.BlockSpec(shape, lambda *_: (0,) * nd, pipeline_mode=pl.Buffered(1))


def _rms(x, g):
    return x * lax.rsqrt(jnp.mean(x * x, axis=-1, keepdims=True) + EPS) * g


def _silu(t):
    return t / (1.0 + jnp.exp(-t))


def _norm_proj_kernel(x_ref, g_ref, w_ref, *rest, segs, col_chunk):
    o_refs, h_ref = rest[:-1], rest[-1]
    h_ref[...] = _rms(x_ref[...], g_ref[...]).astype(BF16)
    for o_ref, (start, width) in zip(o_refs, segs):
        for c in range(0, width, col_chunk):
            cw = min(col_chunk, width - c)
            o_ref[:, c:c + cw] = jnp.dot(h_ref[...], w_ref[:, start + c:start + c + cw],
                                         preferred_element_type=F32).astype(o_ref.dtype)


def _norm_proj(x2, g, w, segs, dtypes, tm=ROW_TILE):
    m, d = x2.shape
    n = w.shape[1]
    out_shape = [jax.ShapeDtypeStruct((m, wd), dt) for (_, wd), dt in zip(segs, dtypes)]
    return pl.pallas_call(
        functools.partial(_norm_proj_kernel, segs=tuple(segs), col_chunk=512),
        grid=(m // tm,),
        in_specs=[pl.BlockSpec((tm, d), lambda i: (i, 0)),
                  pl.BlockSpec((1, d), lambda i: (0, 0)),
                  _resident((d, n))],
        out_specs=[pl.BlockSpec((tm, wd), lambda i: (i, 0)) for (_, wd) in segs],
        out_shape=out_shape,
        scratch_shapes=[pltpu.VMEM((tm, d), BF16)],
        compiler_params=_cparams(1),
        name="norm_proj",
    )(x2, g.reshape(1, d), w)


HALO = BF16_ROWS


def _conv3(t, w, tm):
    n = tm + 2 * HALO
    up = pltpu.roll(t, 1, 0)[HALO:HALO + tm]
    dn = pltpu.roll(t, n - 1, 0)[HALO:HALO + tm]
    return w[0:1] * up + w[1:2] * t[HALO:HALO + tm] + w[2:3] * dn


def _gated_mlp_kernel(xp_ref, x_ref, xn_ref, g_ref, *rest, mode, tm, tiles_per_seq, nchunks,
                      final_norm):
    if mode == "ffn":
        wa_ref, wu_ref, cw_ref, cb_ref, wd_ref, fg_ref, o_ref, h_ref, acc_ref = rest
    else:
        wb_ref, wc_ref, wx_ref, cw_ref, wd_ref, fg_ref, o_ref, h_ref, acc_ref = rest
    pos = pl.program_id(0) % tiles_per_seq
    fprev = (pos != 0).astype(F32)
    fnext = (pos != tiles_per_seq - 1).astype(F32)
    g = g_ref[...]
    h_ref[0:HALO, :] = (_rms(xp_ref[...], g) * fprev).astype(BF16)
    h_ref[HALO:HALO + tm, :] = _rms(x_ref[...], g).astype(BF16)
    h_ref[HALO + tm:, :] = (_rms(xn_ref[...], g) * fnext).astype(BF16)
    acc_ref[...] = jnp.zeros_like(acc_ref)

    def body(c, carry):
        h_ext = h_ref[...]
        h_main = h_ref[HALO:HALO + tm, :]
        if mode == "ffn":
            a = jnp.dot(h_ext, wa_ref[c], preferred_element_type=F32)
            u = jnp.dot(h_main, wu_ref[c], preferred_element_type=F32)
            t = _conv3(a, cw_ref[c], tm) + cb_ref[c]
            gt = _silu(t) * u
        else:
            cg = jnp.dot(h_ext, wc_ref[c], preferred_element_type=F32)
            xi = jnp.dot(h_ext, wx_ref[c], preferred_element_type=F32)
            bg = jnp.dot(h_main, wb_ref[c], preferred_element_type=F32)
            gt = bg * _conv3(cg * xi, cw_ref[c], tm)
        acc_ref[...] += jnp.dot(gt.astype(BF16), wd_ref[c], preferred_element_type=F32)
        return carry

    lax.fori_loop(0, nchunks, body, 0, unroll=True)
    y = x_ref[...] + acc_ref[...]
    if final_norm:
        y = _rms(y, fg_ref[...])
    o_ref[...] = y


def _chunk_cols(w, cw=MID_CHUNK):
    k, n = w.shape
    return w.reshape(k, n // cw, cw).transpose(1, 0, 2)


def _gated_mlp(x2, g, weights, mode, seq, final_g=None, tm=ROW_TILE):
    m, d = x2.shape
    nchunks = weights[-1].shape[0]
    hb = tm // HALO
    n_hblocks = m // HALO
    in_specs = [
        pl.BlockSpec((HALO, d), lambda i: (jnp.maximum(i * hb - 1, 0), 0)),
        pl.BlockSpec((tm, d), lambda i: (i, 0)),
        pl.BlockSpec((HALO, d), lambda i: (jnp.minimum((i + 1) * hb, n_hblocks - 1), 0)),
        pl.BlockSpec((1, d), lambda i: (0, 0)),
    ] + [_resident(w.shape) for w in weights] + [pl.BlockSpec((1, d), lambda i: (0, 0))]
    fg = (final_g if final_g is not None else jnp.ones((d,), F32)).reshape(1, d)
    return pl.pallas_call(
        functools.partial(_gated_mlp_kernel, mode=mode, tm=tm, tiles_per_seq=seq // tm,
                          nchunks=nchunks, final_norm=final_g is not None),
        grid=(m // tm,),
        in_specs=in_specs,
        out_specs=pl.BlockSpec((tm, d), lambda i: (i, 0)),
        out_shape=jax.ShapeDtypeStruct((m, d), F32),
        scratch_shapes=[pltpu.VMEM((tm + 2 * HALO, d), BF16), pltpu.VMEM((tm, d), F32)],
        compiler_params=_cparams(1),
        name="gated_mlp_" + mode,
    )(x2, x2, x2, g.reshape(1, d), *weights, fg)


def _out_proj_kernel(x_ref, o_ref, w_ref, y_ref):
    y_ref[...] = x_ref[...] + jnp.dot(o_ref[...].astype(BF16), w_ref[...],
                                      preferred_element_type=F32)


def _out_proj(x2, o2, w, tm=ROW_TILE):
    m, d = x2.shape
    k = o2.shape[1]
    return pl.pallas_call(
        _out_proj_kernel,
        grid=(m // tm,),
        in_specs=[pl.BlockSpec((tm, d), lambda i: (i, 0)),
                  pl.BlockSpec((tm, k), lambda i: (i, 0)),
                  _resident((k, d))],
        out_specs=pl.BlockSpec((tm, d), lambda i: (i, 0)),
        out_shape=jax.ShapeDtypeStruct((m, d), F32),
        compiler_params=_cparams(1),
        name="out_proj",
    )(x2, o2, w)


def _ssd_out_kernel(x_ref, y_ref, z_ref, g_ref, w_ref, o_ref):
    yv = y_ref[...] * _silu(z_ref[...].astype(F32))
    hn = _rms(yv, g_ref[...]).astype(BF16)
    o_ref[...] = x_ref[...] + jnp.dot(hn, w_ref[...], preferred_element_type=F32)


def _ssd_out(x2, y2, z2, g, w, tm=ROW_TILE):
    m, d = x2.shape
    k = y2.shape[1]
    return pl.pallas_call(
        _ssd_out_kernel,
        grid=(m // tm,),
        in_specs=[pl.BlockSpec((tm, d), lambda i: (i, 0)),
                  pl.BlockSpec((tm, k), lambda i: (i, 0)),
                  pl.BlockSpec((tm, k), lambda i: (i, 0)),
                  pl.BlockSpec((1, k), lambda i: (0, 0)),
                  _resident((k, d))],
        out_specs=pl.BlockSpec((tm, d), lambda i: (i, 0)),
        out_shape=jax.ShapeDtypeStruct((m, d), F32),
        compiler_params=_cparams(1),
        name="ssd_out",
    )(x2, y2, z2, g.reshape(1, k), w)


def _attn_kernel(*refs, q_rows, halo, pair_heads, kv_div, has_sink, has_prev, want_lse, nb):
    refs = list(refs)
    q_ref, kp_ref, k_ref, kn_ref, vp_ref, v_ref, vn_ref, bias_ref = refs[:8]
    refs = refs[8:]
    sink_ref = refs.pop(0) if has_sink else None
    if has_prev:
        op_ref, lp_ref = refs.pop(0), refs.pop(0)
    o_ref = refs.pop(0)
    lse_ref = refs.pop(0) if want_lse else None
    kcat, vcat, pv_ref = refs
    q, h = q_rows, halo
    i = pl.program_id(2)
    variant = jnp.where(i == 0, 0, jnp.where(i == nb - 1, 2, 1))
    kcat[0:h] = kp_ref[0]
    kcat[h:h + q] = k_ref[0]
    kcat[h + q:] = kn_ref[0]
    vcat[0:h] = vp_ref[0]
    vcat[h:h + q] = v_ref[0]
    vcat[h + q:] = vn_ref[0]
    lane = lax.broadcasted_iota(jnp.int32, (q, LANES), 1)
    lo = lane < HEAD_DIM
    lo_b = (lax.broadcasted_iota(jnp.int32, (1, LANES), 1) < HEAD_DIM).astype(BF16)
    hi_b = 1.0 - lo_b
    m_tile = jnp.zeros((q, LANES), F32)
    l_tile = jnp.ones((q, LANES), F32)
    for p, heads in enumerate(pair_heads):
        cols = slice(p * LANES, (p + 1) * LANES)
        kcols = slice((p // kv_div) * LANES, (p // kv_div + 1) * LANES)
        q2 = q_ref[0, :, cols]
        qs = jnp.concatenate([q2 * lo_b, q2 * hi_b], axis=0)
        s = lax.dot_general(qs, kcat[:, kcols], (((1,), (1,)), ((), ())),
                            preferred_element_type=F32) + bias_ref[variant, p]
        m = jnp.max(s, axis=-1, keepdims=True)
        if has_sink:
            sk = jnp.concatenate([jnp.full((q, 1), sink_ref[heads[0]], F32),
                                  jnp.full((q, 1), sink_ref[heads[1]], F32)], axis=0)
            m = jnp.maximum(m, sk)
        e = jnp.exp(s - m)
        l = jnp.sum(e, axis=-1, keepdims=True)
        if has_sink:
            l = l + jnp.exp(sk - m)
        pv_ref[p] = jnp.dot(e.astype(BF16), vcat[:, kcols], preferred_element_type=F32)
        m_tile = jnp.where(lane == 2 * p, m[:q], jnp.where(lane == 2 * p + 1, m[q:], m_tile))
        l_tile = jnp.where(lane == 2 * p, l[:q], jnp.where(lane == 2 * p + 1, l[q:], l_tile))
    w_cur = 1.0 / l_tile
    if want_lse or has_prev:
        lse = m_tile + jnp.log(l_tile)
    if has_prev:
        lp = lp_ref[0]
        n = jnp.maximum(lp, lse)
        n = n + jnp.log(jnp.exp(lp - n) + jnp.exp(lse - n))
        w_cur = w_cur * jnp.exp(lse - n)
        w_prev = jnp.exp(lp - n)
        lse = n
    if want_lse:
        lse_ref[0] = lse
    for p in range(len(pair_heads)):
        cols = slice(p * LANES, (p + 1) * LANES)
        pv = pv_ref[p]
        o = jnp.where(lo, pv[:q] * w_cur[:, 2 * p:2 * p + 1], pv[q:] * w_cur[:, 2 * p + 1:2 * p + 2])
        if has_prev:
            o = o + op_ref[0, :, cols] * jnp.where(lo, w_prev[:, 2 * p:2 * p + 1],
                                                   w_prev[:, 2 * p + 1:2 * p + 2])
        o_ref[0, :, cols] = o.astype(o_ref.dtype)


def _attention(q_arr, k_arr, v_arr, qcol, kcol, vcol, kv_width, bias, length, stride, halo,
               pair_heads, kv_div, sink=None, prev=None, want_lse=False, out_dtype=F32):
    b = q_arr.shape[0]
    q = ATT_Q
    nb = length // q
    assert nb >= 2 and length % q == 0 and q % halo == 0
    qh = q // halo
    n_hblocks = length // halo
    nk = q + 2 * halo
    width = N_HEADS * HEAD_DIM

    def main_map(col):
        return lambda bi, c, i: (bi, i, col(c))

    def prev_map(col):
        return lambda bi, c, i: (bi, jnp.maximum(i * qh - 1, 0), col(c))

    def next_map(col):
        return lambda bi, c, i: (bi, jnp.minimum((i + 1) * qh, n_hblocks - 1), col(c))

    in_specs = [pl.BlockSpec((1, q, width), main_map(qcol))]
    args = [q_arr]
    for arr, col in ((k_arr, kcol), (v_arr, vcol)):
        in_specs += [pl.BlockSpec((1, halo, kv_width), prev_map(col)),
                     pl.BlockSpec((1, q, kv_width), main_map(col)),
                     pl.BlockSpec((1, halo, kv_width), next_map(col))]
        args += [arr, arr, arr]
    in_specs.append(_resident(bias.shape))
    args.append(bias)
    if sink is not None:
        in_specs.append(pl.BlockSpec(memory_space=pltpu.SMEM))
        args.append(sink)
    if prev is not None:
        in_specs += [pl.BlockSpec((1, q, width), main_map(lambda c: c)),
                     pl.BlockSpec((1, q, LANES), main_map(lambda c: c))]
        args += list(prev)
    out_specs = [pl.BlockSpec((1, q, width), main_map(lambda c: c))]
    out_shape = [jax.ShapeDtypeStruct((b, length, stride * width), out_dtype)]
    if want_lse:
        out_specs.append(pl.BlockSpec((1, q, LANES), main_map(lambda c: c)))
        out_shape.append(jax.ShapeDtypeStruct((b, length, stride * LANES), F32))
    return pl.pallas_call(
        functools.partial(_attn_kernel, q_rows=q, halo=halo, pair_heads=tuple(pair_heads),
                          kv_div=kv_div, has_sink=sink is not None, has_prev=prev is not None,
                          want_lse=want_lse, nb=nb),
        grid=(b, stride, nb),
        in_specs=in_specs,
        out_specs=out_specs,
        out_shape=out_shape,
        scratch_shapes=[pltpu.VMEM((nk, kv_width), BF16), pltpu.VMEM((nk, kv_width), BF16),
                        pltpu.VMEM((len(pair_heads), 2 * q, LANES), F32)],
        compiler_params=_cparams(3),
        name="banded_attention",
    )(*args)


def _t5_bucket(rel):
    half = NUM_BUCKETS // 2
    exact = half // 2
    n = np.abs(rel)
    log_ratio = np.log(np.maximum(n, 1) / exact) / math.log(MAX_DISTANCE / exact)
    large = np.minimum(exact + (log_ratio * (half - exact)).astype(np.int64), half - 1)
    return np.where(rel > 0, half, 0) + np.where(n < exact, n, large)


def _attn_bias(rel_bias, halo, stride, pair_heads):
    q = ATT_Q
    nk = q + 2 * halo
    i = np.arange(q)[:, None]
    j = np.arange(nk)[None, :]
    rel = (j - halo) - i
    band = np.abs(rel) <= halo
    table = rel_bias.astype(F32)[_t5_bucket(rel * stride)]
    heads = np.asarray(pair_heads).reshape(-1)
    table = jnp.transpose(table, (2, 0, 1))[heads].reshape(len(pair_heads), 2 * q, nk)
    valid = np.stack([band & (j >= halo), band, band & (j < halo + q)])
    valid = np.concatenate([valid, valid], axis=1)[:, None]
    return jnp.where(valid, table[None], NEG_INF)


SSD_HALO = BF16_ROWS
SSD_EXT = CHUNK + 2 * SSD_HALO


def _ssd_kernel(xp_ref, x_ref, xn_ref, dt_ref, cw_ref, cb_ref, dtb_ref, a_ref, dsk_ref, *rest,
                direction, nc, has_yin):
    if has_yin:
        yin_ref, y_ref, xe_ref, xc_ref, st_ref = rest
    else:
        y_ref, xe_ref, xc_ref, st_ref = rest
    d = direction
    c = pl.program_id(1)
    cc = c if d == 0 else nc - 1 - c
    fprev = (cc != 0).astype(F32)
    fnext = (cc != nc - 1).astype(F32)

    @pl.when(c == 0)
    def _():
        st_ref[...] = jnp.zeros_like(st_ref)

    h0 = SSD_HALO
    xe_ref[0:h0] = xp_ref[0].astype(F32) * fprev
    xe_ref[h0:h0 + CHUNK] = x_ref[0].astype(F32)
    xe_ref[h0 + CHUNK:] = xn_ref[0].astype(F32) * fnext
    cchunk = 512
    for j in range(0, CONV_DIM, cchunk):
        xe = xe_ref[:, j:j + cchunk]
        w = cw_ref[:, j:j + cchunk]
        t = (w[0:1] * pltpu.roll(xe, 2, 0)[h0:h0 + CHUNK]
             + w[1:2] * pltpu.roll(xe, 1, 0)[h0:h0 + CHUNK]
             + w[2:3] * xe[h0:h0 + CHUNK]
             + w[3:4] * pltpu.roll(xe, SSD_EXT - 1, 0)[h0:h0 + CHUNK]) + cb_ref[:, j:j + cchunk]
        xc_ref[:, j:j + cchunk] = _silu(t)

    dtr = dt_ref[0] + dtb_ref[...]
    dt = jnp.maximum(dtr, 0.0) + jnp.log1p(jnp.exp(-jnp.abs(dtr)))
    adt = dt * a_ref[...]
    dt_t = dt.T
    adt_t = adt.T
    li = lax.broadcasted_iota(jnp.int32, (CHUNK, CHUNK), 0)
    ui = lax.broadcasted_iota(jnp.int32, (CHUNK, CHUNK), 1)
    mask = (ui <= li) if d == 0 else (ui >= li)
    tri = mask.astype(F32)
    hp = lax.Precision.HIGHEST
    acs = jnp.dot(tri, adt, precision=hp, preferred_element_type=F32)
    acs_t = lax.dot_general(adt_t, tri, (((1,), (1,)), ((), ())), precision=hp,
                            preferred_element_type=F32)
    tot = jnp.sum(adt_t, axis=1, keepdims=True)
    lo = lax.broadcasted_iota(jnp.int32, (CHUNK, LANES), 1) < HEAD_DIM

    heads_per_group = SSM_HEADS // SSM_GROUPS
    pairs_per_group = heads_per_group // 2
    for grp in range(SSM_GROUPS):
        bcol = D_INNER + grp * D_STATE
        ccol = D_INNER + SSM_GROUPS * D_STATE + grp * D_STATE
        bg = xc_ref[:, bcol:bcol + D_STATE]
        cg = xc_ref[:, ccol:ccol + D_STATE].astype(BF16)
        cb = lax.dot_general(cg, bg.astype(BF16), (((1,), (1,)), ((), ())),
                             preferred_element_type=F32)
        bg_t = bg.T
        for pp in range(pairs_per_group):
            p = grp * pairs_per_group + pp
            cols = slice(p * LANES, (p + 1) * LANES)
            x2 = xc_ref[:, cols]
            x2b = x2.astype(BF16)
            zero = jnp.zeros_like(x2b)
            xbd = jnp.concatenate([jnp.where(lo, x2b, zero), jnp.where(lo, zero, x2b)], axis=0)
            ms, bts, es, cds = [], [], [], []
            for hh in (d * SSM_HEADS + 2 * p, d * SSM_HEADS + 2 * p + 1):
                a_col = jnp.broadcast_to(acs[:, hh:hh + 1], (CHUNK, CHUNK))
                a_row = acs_t[hh:hh + 1, :]
                dt_row = dt_t[hh:hh + 1, :]
                lmat = jnp.exp(jnp.where(mask, a_col - a_row, NEG_INF))
                ms.append((cb * lmat * dt_row).astype(BF16))
                w_row = jnp.exp(tot[hh:hh + 1, :] - a_row) * dt_row
                bts.append((bg_t * w_row).astype(BF16))
                es.append(jnp.exp(a_col))
                cds.append(jnp.exp(tot[hh:hh + 1, :]))
            st = st_ref[p]
            y = jnp.dot(jnp.concatenate(ms, axis=1), xbd, preferred_element_type=F32)
            y = y + jnp.dot(cg, st.astype(BF16), preferred_element_type=F32) * jnp.where(lo, es[0], es[1])
            if d == 0:
                y = y + x2 * dsk_ref[:, cols]
            if has_yin:
                y = y + yin_ref[0, :, cols]
            y_ref[0, :, cols] = y
            st_ref[p] = (st * jnp.where(lo, cds[0], cds[1])
                         + jnp.dot(jnp.concatenate(bts, axis=1), xbd, preferred_element_type=F32))


def _ssd(xbc, dt_raw, conv_w, conv_b, dt_bias, a_neg, d_skip, direction, y_in=None):
    b, s, _ = xbc.shape
    nc = s // CHUNK
    hb = CHUNK // SSD_HALO
    n_hblocks = s // SSD_HALO

    def cc(c):
        return c if direction == 0 else nc - 1 - c

    in_specs = [
        pl.BlockSpec((1, SSD_HALO, CONV_DIM), lambda bi, c: (bi, jnp.maximum(cc(c) * hb - 1, 0), 0)),
        pl.BlockSpec((1, CHUNK, CONV_DIM), lambda bi, c: (bi, cc(c), 0)),
        pl.BlockSpec((1, SSD_HALO, CONV_DIM),
                     lambda bi, c: (bi, jnp.minimum((cc(c) + 1) * hb, n_hblocks - 1), 0)),
        pl.BlockSpec((1, CHUNK, LANES), lambda bi, c: (bi, cc(c), 0)),
        pl.BlockSpec(conv_w.shape, lambda bi, c: (0, 0)),
        pl.BlockSpec((1, CONV_DIM), lambda bi, c: (0, 0)),
        pl.BlockSpec((1, LANES), lambda bi, c: (0, 0)),
        pl.BlockSpec((1, LANES), lambda bi, c: (0, 0)),
        pl.BlockSpec((1, D_INNER), lambda bi, c: (0, 0)),
    ]
    args = [xbc, xbc, xbc, dt_raw, conv_w, conv_b.reshape(1, CONV_DIM), dt_bias, a_neg, d_skip]
    if y_in is not None:
        in_specs.append(pl.BlockSpec((1, CHUNK, D_INNER), lambda bi, c: (bi, cc(c), 0)))
        args.append(y_in)
    return pl.pallas_call(
        functools.partial(_ssd_kernel, direction=direction, nc=nc, has_yin=y_in is not None),
        grid=(b, nc),
        in_specs=in_specs,
        out_specs=pl.BlockSpec((1, CHUNK, D_INNER), lambda bi, c: (bi, cc(c), 0)),
        out_shape=jax.ShapeDtypeStruct((b, s, D_INNER), F32),
        scratch_shapes=[pltpu.VMEM((SSD_EXT, CONV_DIM), F32),
                        pltpu.VMEM((CHUNK, CONV_DIM), F32),
                        pltpu.VMEM((SSM_HEADS // 2, D_STATE, LANES), F32)],
        compiler_params=_cparams(2),
        name="ssd_scan",
    )(*args)


A_PAIR_HEADS = tuple((2 * p, 2 * p + 1) for p in range(N_HEADS // 2))
B_PAIR_HEADS = tuple((8 * t + j, 8 * t + 4 + j) for t in range(2) for j in range(4))


def _prepare(p):
    aw = N_HEADS * HEAD_DIM
    scale = HEAD_DIM ** -0.5
    out = {}
    wa = p["a_w_qkv"][0].reshape(D_MODEL, N_DIL, 3, aw)
    wa = wa * jnp.asarray([scale, 1.0, 1.0], F32)[None, None, :, None]
    out["a_w_qkv"] = [wa[:, gi].reshape(D_MODEL, 3 * aw).astype(BF16) for gi in range(N_DIL)]
    out["a_w_o"] = p["a_w_o"][0].astype(BF16)
    perm = np.asarray(B_PAIR_HEADS).reshape(-1)
    cols = (perm[:, None] * HEAD_DIM + np.arange(HEAD_DIM)[None, :]).reshape(-1)
    wb = p["b_w_qkv"][0]
    out["b_w_qkv"] = jnp.concatenate([wb[:, :aw][:, cols] * scale, wb[:, aw:]], axis=1).astype(BF16)
    out["b_w_o"] = p["b_w_o"][0][cols, :].astype(BF16)
    out["b_sink"] = p["b_sink"][0].astype(F32)
    wc = p["c_w_in"][0].astype(BF16)
    out["c_w"] = (_chunk_cols(wc[:, :D_MODEL]), _chunk_cols(wc[:, D_MODEL:2 * D_MODEL]),
                  _chunk_cols(wc[:, 2 * D_MODEL:]), _chunk_cols(p["c_conv_w"][0].astype(F32)),
                  p["c_w_out"][0].astype(BF16).reshape(D_MODEL // MID_CHUNK, MID_CHUNK, D_MODEL))
    wd = p["d_w_in"][0]
    pad = jnp.zeros((D_MODEL, LANES - 2 * SSM_HEADS), F32)
    out["d_w_in"] = jnp.concatenate([wd, pad], axis=1).astype(BF16)
    lane_pad = jnp.zeros((LANES - 2 * SSM_HEADS,), F32)
    out["d_conv_w"] = p["d_conv_w"][0].astype(F32)
    out["d_conv_b"] = p["d_conv_b"][0].astype(F32)
    out["d_dt_bias"] = jnp.concatenate([p["d_dt_bias"][0].reshape(-1).astype(F32), lane_pad]).reshape(1, LANES)
    out["d_a_neg"] = jnp.concatenate([-jnp.exp(p["d_a_log"][0].reshape(-1).astype(F32)), lane_pad]).reshape(1, LANES)
    out["d_skip"] = jnp.repeat(p["d_skip"][0].astype(F32), HEAD_DIM).reshape(1, D_INNER)
    out["d_norm_g"] = p["d_norm_g"][0].astype(F32)
    out["d_w_out"] = p["d_w_out"][0].astype(BF16)
    ffn = []
    for i in range(DEPTH):
        wu = p["ffn_w_up"][i].astype(BF16)
        ffn.append((_chunk_cols(wu[:, :D_FF]), _chunk_cols(wu[:, D_FF:]),
                    _chunk_cols(p["ffn_conv_w"][i].astype(F32)),
                    _chunk_cols(p["ffn_conv_b"][i].astype(F32).reshape(1, D_FF)),
                    p["ffn_w_down"][i].astype(BF16).reshape(D_FF // MID_CHUNK, MID_CHUNK, D_MODEL)))
    out["ffn"] = ffn
    out["a_bias"] = [_attn_bias(p["rel_bias"], window // (2 * r), r, A_PAIR_HEADS)
                     for window, r in DILATIONS]
    out["b_bias"] = _attn_bias(p["rel_bias"], B_HALF_WINDOW, 1, B_PAIR_HEADS)
    out["norm_g"] = p["norm_g"].astype(F32)
    out["final_g"] = p["final_g"].astype(F32)
    return out


def _mixer_dilated(x2, g, w, bsz, seq):
    aw = N_HEADS * HEAD_DIM
    d = x2.shape[1]
    prev = None
    prev_r = 1
    for gi, (window, r) in enumerate(DILATIONS):
        length = seq // r
        xr = x2 if r == 1 else x2.reshape(bsz, length, r, d).transpose(0, 2, 1, 3).reshape(bsz * seq, d)
        (qkv,) = _norm_proj(xr, g, w["a_w_qkv"][gi], [(0, 3 * aw)], [BF16])
        view = qkv.reshape(bsz * r, length, 3 * aw)
        if prev is not None:
            k = r // prev_r

            def regroup(t):
                t = t.reshape(bsz, prev_r, length, k, t.shape[-1]).transpose(0, 3, 1, 2, 4)
                return t.reshape(bsz * r, length, t.shape[-1])

            prev = (regroup(prev[0]), regroup(prev[1]))
        last = gi == N_DIL - 1
        res = _attention(
            view, view, view, qcol=lambda c: 0, kcol=lambda c: 1, vcol=lambda c: 2,
            kv_width=aw, bias=w["a_bias"][gi], length=length, stride=1, halo=window // (2 * r),
            pair_heads=A_PAIR_HEADS, kv_div=1, prev=prev, want_lse=not last, out_dtype=F32)
        prev = tuple(res)
        prev_r = r
    o = prev[0].reshape(bsz, prev_r, seq // prev_r, aw).transpose(0, 2, 1, 3)
    return _out_proj(x2, o.reshape(bsz * seq, aw), w["a_w_o"])


def _mixer_window(x2, g, w, bsz, seq):
    aw = N_HEADS * HEAD_DIM
    kvw = B_KV_HEADS * HEAD_DIM
    n = aw + 2 * kvw
    (qkv,) = _norm_proj(x2, g, w["b_w_qkv"], [(0, n)], [BF16])
    view = qkv.reshape(bsz, seq, n)
    (o,) = _attention(view[:, :, :aw], view, view,
                      qcol=lambda c: 0, kcol=lambda c: aw // kvw, vcol=lambda c: aw // kvw + 1,
                      kv_width=kvw, bias=w["b_bias"], length=seq, stride=1, halo=B_HALF_WINDOW,
                      pair_heads=B_PAIR_HEADS, kv_div=4, sink=w["b_sink"], out_dtype=BF16)
    return _out_proj(x2, o.reshape(bsz * seq, aw), w["b_w_o"])


def _mixer_ssd(x2, g, w, bsz, seq):
    z, xbc, dt_raw = _norm_proj(
        x2, g, w["d_w_in"],
        [(0, D_INNER), (D_INNER, CONV_DIM), (D_INNER + CONV_DIM, LANES)], [BF16, BF16, F32])
    xbc = xbc.reshape(bsz, seq, CONV_DIM)
    dt_raw = dt_raw.reshape(bsz, seq, LANES)
    common = (w["d_conv_w"], w["d_conv_b"], w["d_dt_bias"], w["d_a_neg"], w["d_skip"])
    y = _ssd(xbc, dt_raw, *common, direction=0)
    y = _ssd(xbc, dt_raw, *common, direction=1, y_in=y)
    return _ssd_out(x2, y.reshape(bsz * seq, D_INNER), z, w["d_norm_g"], w["d_w_out"])


def _trunk(x, w):
    bsz, seq, d = x.shape
    x2 = x.reshape(bsz * seq, d)
    ng = w["norm_g"]
    for i in range(DEPTH):
        kind = i % 4
        if kind == 0:
            x2 = _mixer_dilated(x2, ng[i, 0], w, bsz, seq)
        elif kind == 1:
            x2 = _mixer_window(x2, ng[i, 0], w, bsz, seq)
        elif kind == 2:
            x2 = _gated_mlp(x2, ng[i, 0], w["c_w"], "sconv", seq)
        else:
            x2 = _mixer_ssd(x2, ng[i, 0], w, bsz, seq)
        x2 = _gated_mlp(x2, ng[i, 1], w["ffn"][i], "ffn", seq,
                        final_g=w["final_g"] if i == DEPTH - 1 else None)
    return x2.reshape(bsz, seq, d)


def kernel(x_prompt, x_sample, rel_bias, a_w_qkv, a_w_o, b_w_qkv, b_sink, b_w_o, c_w_in, c_conv_w,
           c_w_out, d_w_in, d_conv_w, d_conv_b, d_dt_bias, d_a_log, d_skip, d_norm_g, d_w_out,
           ffn_w_up, ffn_conv_w, ffn_conv_b, ffn_w_down, norm_g, final_g):
    w = _prepare(dict(
        rel_bias=rel_bias, a_w_qkv=a_w_qkv, a_w_o=a_w_o, b_w_qkv=b_w_qkv, b_sink=b_sink, b_w_o=b_w_o,
        c_w_in=c_w_in, c_conv_w=c_conv_w, c_w_out=c_w_out, d_w_in=d_w_in, d_conv_w=d_conv_w,
        d_conv_b=d_conv_b, d_dt_bias=d_dt_bias, d_a_log=d_a_log, d_skip=d_skip, d_norm_g=d_norm_g,
        d_w_out=d_w_out, ffn_w_up=ffn_w_up, ffn_conv_w=ffn_conv_w, ffn_conv_b=ffn_conv_b,
        ffn_w_down=ffn_w_down, norm_g=norm_g, final_g=final_g))
    return (_trunk(x_prompt, w), _trunk(x_sample, w))
```

```python
import functools
import math

import numpy as np
import jax
import jax.numpy as jnp
from jax import lax
from jax.experimental import pallas as pl
from jax.experimental.pallas import tpu as pltpu

F32 = jnp.float32
BF16 = jnp.bfloat16

D_MODEL = 1024
DEPTH = 4
HEAD_DIM = 64
N_HEADS = 16
DILATIONS = ((128, 1), (512, 4), (2048, 16))
N_DIL = 3
B_KV_HEADS = 4
B_HALF_WINDOW = 128
NUM_BUCKETS = 32
MAX_DISTANCE = 1024
D_INNER = 2 * D_MODEL
SSM_HEADS = 32
SSM_GROUPS = 4
D_STATE = 128
CHUNK = 128
CONV_DIM = D_INNER + 2 * SSM_GROUPS * D_STATE
D_FF = 2816
EPS = 1e-6
NEG_INF = -1e30

LANES = 128
BF16_ROWS = 16
VMEM_LIMIT = 56 * 1024 * 1024
ROW_TILE = 512
MID_CHUNK = 256
ATT_Q = 128


def _cparams(n_axes):
    return pltpu.CompilerParams(dimension_semantics=("arbitrary",) * n_axes,
                                vmem_limit_bytes=VMEM_LIMIT)


def _resident(shape):
    nd = len(shape)
    return pl.BlockSpec(shape, lambda *_: (0,) * nd, pipeline_mode=pl.Buffered(1))


def _rms(x, g):
    return x * lax.rsqrt(jnp.mean(x * x, axis=-1, keepdims=True) + EPS) * g


def _silu(t):
    return t / (1.0 + jnp.exp(-t))


def _norm_proj_kernel(x_ref, g_ref, w_ref, *rest, segs, col_chunk):
    o_refs, h_ref = rest[:-1], rest[-1]
    h_ref[...] = _rms(x_ref[...], g_ref[...]).astype(BF16)
    for o_ref, (start, width) in zip(o_refs, segs):
        for c in range(0, width, col_chunk):
            cw = min(col_chunk, width - c)
            o_ref[:, c:c + cw] = jnp.dot(h_ref[...], w_ref[:, start + c:start + c + cw],
                                         preferred_element_type=F32).astype(o_ref.dtype)


def _norm_proj(x2, g, w, segs, dtypes, tm=ROW_TILE):
    m, d = x2.shape
    n = w.shape[1]
    out_shape = [jax.ShapeDtypeStruct((m, wd), dt) for (_, wd), dt in zip(segs, dtypes)]
    return pl.pallas_call(
        functools.partial(_norm_proj_kernel, segs=tuple(segs), col_chunk=512),
        grid=(m // tm,),
        in_specs=[pl.BlockSpec((tm, d), lambda i: (i, 0)),
                  pl.BlockSpec((1, d), lambda i: (0, 0)),
                  _resident((d, n))],
        out_specs=[pl.BlockSpec((tm, wd), lambda i: (i, 0)) for (_, wd) in segs],
        out_shape=out_shape,
        scratch_shapes=[pltpu.VMEM((tm, d), BF16)],
        compiler_params=_cparams(1),
        name="norm_proj",
    )(x2, g.reshape(1, d), w)


HALO = BF16_ROWS


def _conv3(t, w, tm):
    n = tm + 2 * HALO
    up = pltpu.roll(t, 1, 0)[HALO:HALO + tm]
    dn = pltpu.roll(t, n - 1, 0)[HALO:HALO + tm]
    return w[0:1] * up + w[1:2] * t[HALO:HALO + tm] + w[2:3] * dn


def _gated_mlp_kernel(xp_ref, x_ref, xn_ref, g_ref, *rest, mode, tm, tiles_per_seq, nchunks,
                      final_norm):
    if mode == "ffn":
        wa_ref, wu_ref, cw_ref, cb_ref, wd_ref, fg_ref, o_ref, h_ref, acc_ref = rest
    else:
        wb_ref, wc_ref, wx_ref, cw_ref, wd_ref, fg_ref, o_ref, h_ref, acc_ref = rest
    pos = pl.program_id(0) % tiles_per_seq
    fprev = (pos != 0).astype(F32)
    fnext = (pos != tiles_per_seq - 1).astype(F32)
    g = g_ref[...]
    h_ref[0:HALO, :] = (_rms(xp_ref[...], g) * fprev).astype(BF16)
    h_ref[HALO:HALO + tm, :] = _rms(x_ref[...], g).astype(BF16)
    h_ref[HALO + tm:, :] = (_rms(xn_ref[...], g) * fnext).astype(BF16)
    acc_ref[...] = jnp.zeros_like(acc_ref)

    def body(c, carry):
        h_ext = h_ref[...]
        h_main = h_ref[HALO:HALO + tm, :]
        if mode == "ffn":
            a = jnp.dot(h_ext, wa_ref[c], preferred_element_type=F32)
            u = jnp.dot(h_main, wu_ref[c], preferred_element_type=F32)
            t = _conv3(a, cw_ref[c], tm) + cb_ref[c]
            gt = _silu(t) * u
        else:
            cg = jnp.dot(h_ext, wc_ref[c], preferred_element_type=F32)
            xi = jnp.dot(h_ext, wx_ref[c], preferred_element_type=F32)
            bg = jnp.dot(h_main, wb_ref[c], preferred_element_type=F32)
            gt = bg * _conv3(cg * xi, cw_ref[c], tm)
        acc_ref[...] += jnp.dot(gt.astype(BF16), wd_ref[c], preferred_element_type=F32)
        return carry

    lax.fori_loop(0, nchunks, body, 0, unroll=True)
    y = x_ref[...] + acc_ref[...]
    if final_norm:
        y = _rms(y, fg_ref[...])
    o_ref[...] = y


def _chunk_cols(w, cw=MID_CHUNK):
    k, n = w.shape
    return w.reshape(k, n // cw, cw).transpose(1, 0, 2)


def _gated_mlp(x2, g, weights, mode, seq, final_g=None, tm=ROW_TILE):
    m, d = x2.shape
    nchunks = weights[-1].shape[0]
    hb = tm // HALO
    n_hblocks = m // HALO
    in_specs = [
        pl.BlockSpec((HALO, d), lambda i: (jnp.maximum(i * hb - 1, 0), 0)),
        pl.BlockSpec((tm, d), lambda i: (i, 0)),
        pl.BlockSpec((HALO, d), lambda i: (jnp.minimum((i + 1) * hb, n_hblocks - 1), 0)),
        pl.BlockSpec((1, d), lambda i: (0, 0)),
    ] + [_resident(w.shape) for w in weights] + [pl.BlockSpec((1, d), lambda i: (0, 0))]
    fg = (final_g if final_g is not None else jnp.ones((d,), F32)).reshape(1, d)
    return pl.pallas_call(
        functools.partial(_gated_mlp_kernel, mode=mode, tm=tm, tiles_per_seq=seq // tm,
                          nchunks=nchunks, final_norm=final_g is not None),
        grid=(m // tm,),
        in_specs=in_specs,
        out_specs=pl.BlockSpec((tm, d), lambda i: (i, 0)),
        out_shape=jax.ShapeDtypeStruct((m, d), F32),
        scratch_shapes=[pltpu.VMEM((tm + 2 * HALO, d), BF16), pltpu.VMEM((tm, d), F32)],
        compiler_params=_cparams(1),
        name="gated_mlp_" + mode,
    )(x2, x2, x2, g.reshape(1, d), *weights, fg)


def _out_proj_kernel(x_ref, o_ref, w_ref, y_ref):
    y_ref[...] = x_ref[...] + jnp.dot(o_ref[...].astype(BF16), w_ref[...],
                                      preferred_element_type=F32)


def _out_proj(x2, o2, w, tm=ROW_TILE):
    m, d = x2.shape
    k = o2.shape[1]
    return pl.pallas_call(
        _out_proj_kernel,
        grid=(m // tm,),
        in_specs=[pl.BlockSpec((tm, d), lambda i: (i, 0)),
                  pl.BlockSpec((tm, k), lambda i: (i, 0)),
                  _resident((k, d))],
        out_specs=pl.BlockSpec((tm, d), lambda i: (i, 0)),
        out_shape=jax.ShapeDtypeStruct((m, d), F32),
        compiler_params=_cparams(1),
        name="out_proj",
    )(x2, o2, w)


def _ssd_out_kernel(x_ref, y_ref, z_ref, g_ref, w_ref, o_ref):
    yv = y_ref[...] * _silu(z_ref[...].astype(F32))
    hn = _rms(yv, g_ref[...]).astype(BF16)
    o_ref[...] = x_ref[...] + jnp.dot(hn, w_ref[...], preferred_element_type=F32)


def _ssd_out(x2, y2, z2, g, w, tm=ROW_TILE):
    m, d = x2.shape
    k = y2.shape[1]
    return pl.pallas_call(
        _ssd_out_kernel,
        grid=(m // tm,),
        in_specs=[pl.BlockSpec((tm, d), lambda i: (i, 0)),
                  pl.BlockSpec((tm, k), lambda i: (i, 0)),
                  pl.BlockSpec((tm, k), lambda i: (i, 0)),
                  pl.BlockSpec((1, k), lambda i: (0, 0)),
                  _resident((k, d))],
        out_specs=pl.BlockSpec((tm, d), lambda i: (i, 0)),
        out_shape=jax.ShapeDtypeStruct((m, d), F32),
        compiler_params=_cparams(1),
        name="ssd_out",
    )(x2, y2, z2, g.reshape(1, k), w)


def _attn_kernel(*refs, q_rows, halo, pair_heads, kv_div, has_sink, has_prev, want_lse, nb):
    refs = list(refs)
    q_ref, kp_ref, k_ref, kn_ref, vp_ref, v_ref, vn_ref, bias_ref = refs[:8]
    refs = refs[8:]
    sink_ref = refs.pop(0) if has_sink else None
    if has_prev:
        op_ref, lp_ref = refs.pop(0), refs.pop(0)
    o_ref = refs.pop(0)
    lse_ref = refs.pop(0) if want_lse else None
    kcat, vcat, pv_ref = refs
    q, h = q_rows, halo
    i = pl.program_id(2)
    variant = jnp.where(i == 0, 0, jnp.where(i == nb - 1, 2, 1))
    kcat[0:h] = kp_ref[0]
    kcat[h:h + q] = k_ref[0]
    kcat[h + q:] = kn_ref[0]
    vcat[0:h] = vp_ref[0]
    vcat[h:h + q] = v_ref[0]
    vcat[h + q:] = vn_ref[0]
    lane = lax.broadcasted_iota(jnp.int32, (q, LANES), 1)
    lo = lane < HEAD_DIM
    lo_b = (lax.broadcasted_iota(jnp.int32, (1, LANES), 1) < HEAD_DIM).astype(BF16)
    hi_b = 1.0 - lo_b
    m_tile = jnp.zeros((q, LANES), F32)
    l_tile = jnp.ones((q, LANES), F32)
    for p, heads in enumerate(pair_heads):
        cols = slice(p * LANES, (p + 1) * LANES)
        kcols = slice((p // kv_div) * LANES, (p // kv_div + 1) * LANES)
        q2 = q_ref[0, :, cols]
        qs = jnp.concatenate([q2 * lo_b, q2 * hi_b], axis=0)
        s = lax.dot_general(qs, kcat[:, kcols], (((1,), (1,)), ((), ())),
                            preferred_element_type=F32) + bias_ref[variant, p]
        m = jnp.max(s, axis=-1, keepdims=True)
        if has_sink:
            sk = jnp.concatenate([jnp.full((q, 1), sink_ref[heads[0]], F32),
                                  jnp.full((q, 1), sink_ref[heads[1]], F32)], axis=0)
            m = jnp.maximum(m, sk)
        e = jnp.exp(s - m)
        l = jnp.sum(e, axis=-1, keepdims=True)
        if has_sink:
            l = l + jnp.exp(sk - m)
        pv_ref[p] = jnp.dot(e.astype(BF16), vcat[:, kcols], preferred_element_type=F32)
        m_tile = jnp.where(lane == 2 * p, m[:q], jnp.where(lane == 2 * p + 1, m[q:], m_tile))
        l_tile = jnp.where(lane == 2 * p, l[:q], jnp.where(lane == 2 * p + 1, l[q:], l_tile))
    w_cur = 1.0 / l_tile
    if want_lse or has_prev:
        lse = m_tile + jnp.log(l_tile)
    if has_prev:
        lp = lp_ref[0]
        n = jnp.maximum(lp, lse)
        n = n + jnp.log(jnp.exp(lp - n) + jnp.exp(lse - n))
        w_cur = w_cur * jnp.exp(lse - n)
        w_prev = jnp.exp(lp - n)
        lse = n
    if want_lse:
        lse_ref[0] = lse
    for p in range(len(pair_heads)):
        cols = slice(p * LANES, (p + 1) * LANES)
        pv = pv_ref[p]
        o = jnp.where(lo, pv[:q] * w_cur[:, 2 * p:2 * p + 1], pv[q:] * w_cur[:, 2 * p + 1:2 * p + 2])
        if has_prev:
            o = o + op_ref[0, :, cols] * jnp.where(lo, w_prev[:, 2 * p:2 * p + 1],
                                                   w_prev[:, 2 * p + 1:2 * p + 2])
        o_ref[0, :, cols] = o.astype(o_ref.dtype)


def _attention(q_arr, k_arr, v_arr, qcol, kcol, vcol, kv_width, bias, length, stride, halo,
               pair_heads, kv_div, sink=None, prev=None, want_lse=False, out_dtype=F32):
    b = q_arr.shape[0]
    q = ATT_Q
    nb = length // q
    assert nb >= 2 and length % q == 0 and q % halo == 0
    qh = q // halo
    n_hblocks = length // halo
    nk = q + 2 * halo
    width = N_HEADS * HEAD_DIM

    def main_map(col):
        return lambda bi, c, i: (bi, i, col(c))

    def prev_map(col):
        return lambda bi, c, i: (bi, jnp.maximum(i * qh - 1, 0), col(c))

    def next_map(col):
        return lambda bi, c, i: (bi, jnp.minimum((i + 1) * qh, n_hblocks - 1), col(c))

    in_specs = [pl.BlockSpec((1, q, width), main_map(qcol))]
    args = [q_arr]
    for arr, col in ((k_arr, kcol), (v_arr, vcol)):
        in_specs += [pl.BlockSpec((1, halo, kv_width), prev_map(col)),
                     pl.BlockSpec((1, q, kv_width), main_map(col)),
                     pl.BlockSpec((1, halo, kv_width), next_map(col))]
        args += [arr, arr, arr]
    in_specs.append(_resident(bias.shape))
    args.append(bias)
    if sink is not None:
        in_specs.append(pl.BlockSpec(memory_space=pltpu.SMEM))
        args.append(sink)
    if prev is not None:
        in_specs += [pl.BlockSpec((1, q, width), main_map(lambda c: c)),
                     pl.BlockSpec((1, q, LANES), main_map(lambda c: c))]
        args += list(prev)
    out_specs = [pl.BlockSpec((1, q, width), main_map(lambda c: c))]
    out_shape = [jax.ShapeDtypeStruct((b, length, stride * width), out_dtype)]
    if want_lse:
        out_specs.append(pl.BlockSpec((1, q, LANES), main_map(lambda c: c)))
        out_shape.append(jax.ShapeDtypeStruct((b, length, stride * LANES), F32))
    return pl.pallas_call(
        functools.partial(_attn_kernel, q_rows=q, halo=halo, pair_heads=tuple(pair_heads),
                          kv_div=kv_div, has_sink=sink is not None, has_prev=prev is not None,
                          want_lse=want_lse, nb=nb),
        grid=(b, stride, nb),
        in_specs=in_specs,
        out_specs=out_specs,
        out_shape=out_shape,
        scratch_shapes=[pltpu.VMEM((nk, kv_width), BF16), pltpu.VMEM((nk, kv_width), BF16),
                        pltpu.VMEM((len(pair_heads), 2 * q, LANES), F32)],
        compiler_params=_cparams(3),
        name="banded_attention",
    )(*args)


def _t5_bucket(rel):
    half = NUM_BUCKETS // 2
    exact = half // 2
    n = np.abs(rel)
    log_ratio = np.log(np.maximum(n, 1) / exact) / math.log(MAX_DISTANCE / exact)
    large = np.minimum(exact + (log_ratio * (half - exact)).astype(np.int64), half - 1)
    return np.where(rel > 0, half, 0) + np.where(n < exact, n, large)


def _attn_bias(rel_bias, halo, stride, pair_heads):
    q = ATT_Q
    nk = q + 2 * halo
    i = np.arange(q)[:, None]
    j = np.arange(nk)[None, :]
    rel = (j - halo) - i
    band = np.abs(rel) <= halo
    table = rel_bias.astype(F32)[_t5_bucket(rel * stride)]
    heads = np.asarray(pair_heads).reshape(-1)
    table = jnp.transpose(table, (2, 0, 1))[heads].reshape(len(pair_heads), 2 * q, nk)
    valid = np.stack([band & (j >= halo), band, band & (j < halo + q)])
    valid = np.concatenate([valid, valid], axis=1)[:, None]
    return jnp.where(valid, table[None], NEG_INF)


SSD_HALO = BF16_ROWS
SSD_EXT = CHUNK + 2 * SSD_HALO


def _ssd_kernel(*refs, direction, nc):
    d = direction
    if d == 0:
        (xp_ref, x_ref, xn_ref, cw_ref, cb_ref, dsk_ref, dt_ref, dtb_ref, a_ref,
         y_ref, xo_ref, xe_ref, xc_ref, st_ref) = refs
    else:
        xi_ref, yin_ref, dt_ref, dtb_ref, a_ref, y_ref, xc_ref, st_ref = refs
    c = pl.program_id(1)

    @pl.when(c == 0)
    def _():
        st_ref[...] = jnp.zeros_like(st_ref)

    if d == 0:
        fprev = (c != 0).astype(F32)
        fnext = (c != nc - 1).astype(F32)
        h0 = SSD_HALO
        xe_ref[0:h0] = xp_ref[0].astype(F32) * fprev
        xe_ref[h0:h0 + CHUNK] = x_ref[0].astype(F32)
        xe_ref[h0 + CHUNK:] = xn_ref[0].astype(F32) * fnext
        cchunk = 512
        for j in range(0, CONV_DIM, cchunk):
            xe = xe_ref[:, j:j + cchunk]
            w = cw_ref[:, j:j + cchunk]
            t = (w[0:1] * pltpu.roll(xe, 2, 0)[h0:h0 + CHUNK]
                 + w[1:2] * pltpu.roll(xe, 1, 0)[h0:h0 + CHUNK]
                 + w[2:3] * xe[h0:h0 + CHUNK]
                 + w[3:4] * pltpu.roll(xe, SSD_EXT - 1, 0)[h0:h0 + CHUNK]) + cb_ref[:, j:j + cchunk]
            xc = _silu(t)
            xc_ref[:, j:j + cchunk] = xc
            xo_ref[0, :, j:j + cchunk] = xc.astype(xo_ref.dtype)
    else:
        xc_ref[...] = xi_ref[0].astype(F32)

    dtr = dt_ref[0] + dtb_ref[...]
    dt = jnp.maximum(dtr, 0.0) + jnp.log1p(jnp.exp(-jnp.abs(dtr)))
    adt = dt * a_ref[...]
    dt_t = dt.T
    adt_t = adt.T
    li = lax.broadcasted_iota(jnp.int32, (CHUNK, CHUNK), 0)
    ui = lax.broadcasted_iota(jnp.int32, (CHUNK, CHUNK), 1)
    mask = (ui <= li) if d == 0 else (ui >= li)
    tri = mask.astype(F32)
    hp = lax.Precision.HIGHEST
    acs = jnp.dot(tri, adt, precision=hp, preferred_element_type=F32)
    acs_t = lax.dot_general(adt_t, tri, (((1,), (1,)), ((), ())), precision=hp,
                            preferred_element_type=F32)
    tot = jnp.sum(adt_t, axis=1, keepdims=True)
    lo = lax.broadcasted_iota(jnp.int32, (CHUNK, LANES), 1) < HEAD_DIM

    heads_per_group = SSM_HEADS // SSM_GROUPS
    pairs_per_group = heads_per_group // 2
    for grp in range(SSM_GROUPS):
        bcol = D_INNER + grp * D_STATE
        ccol = D_INNER + SSM_GROUPS * D_STATE + grp * D_STATE
        bg = xc_ref[:, bcol:bcol + D_STATE]
        cg = xc_ref[:, ccol:ccol + D_STATE].astype(BF16)
        cb = lax.dot_general(cg, bg.astype(BF16), (((1,), (1,)), ((), ())),
                             preferred_element_type=F32)
        bg_t = bg.T
        for pp in range(pairs_per_group):
            p = grp * pairs_per_group + pp
            cols = slice(p * LANES, (p + 1) * LANES)
            x2 = xc_ref[:, cols]
            x2b = x2.astype(BF16)
            zero = jnp.zeros_like(x2b)
            xbd = jnp.concatenate([jnp.where(lo, x2b, zero), jnp.where(lo, zero, x2b)], axis=0)
            ms, bts, es, cds = [], [], [], []
            for hh in (d * SSM_HEADS + 2 * p, d * SSM_HEADS + 2 * p + 1):
                a_col = jnp.broadcast_to(acs[:, hh:hh + 1], (CHUNK, CHUNK))
                a_row = acs_t[hh:hh + 1, :]
                dt_row = dt_t[hh:hh + 1, :]
                lmat = jnp.exp(jnp.where(mask, a_col - a_row, NEG_INF))
                ms.append((cb * lmat * dt_row).astype(BF16))
                w_row = jnp.exp(tot[hh:hh + 1, :] - a_row) * dt_row
                bts.append((bg_t * w_row).astype(BF16))
                es.append(jnp.exp(a_col))
                cds.append(jnp.exp(tot[hh:hh + 1, :]))
            st = st_ref[p]
            y = jnp.dot(jnp.concatenate(ms, axis=1), xbd, preferred_element_type=F32)
            y = y + jnp.dot(cg, st.astype(BF16), preferred_element_type=F32) * jnp.where(lo, es[0], es[1])
            if d == 0:
                y = y + x2 * dsk_ref[:, cols]
            else:
                y = y + yin_ref[0, :, cols]
            y_ref[0, :, cols] = y
            st_ref[p] = (st * jnp.where(lo, cds[0], cds[1])
                         + jnp.dot(jnp.concatenate(bts, axis=1), xbd, preferred_element_type=F32))


def _ssd(x_in, dt_raw, dt_bias, a_neg, direction, conv=None, y_in=None):
    b, s, _ = x_in.shape
    nc = s // CHUNK
    hb = CHUNK // SSD_HALO
    n_hblocks = s // SSD_HALO

    def chunk_map(bi, c):
        return (bi, c if direction == 0 else nc - 1 - c, 0)

    def const_map(bi, c):
        return (0, 0)

    small = [pl.BlockSpec((1, CHUNK, LANES), chunk_map),
             pl.BlockSpec((1, LANES), const_map), pl.BlockSpec((1, LANES), const_map)]
    y_spec = pl.BlockSpec((1, CHUNK, D_INNER), chunk_map)
    x_spec = pl.BlockSpec((1, CHUNK, CONV_DIM), chunk_map)
    y_shape = jax.ShapeDtypeStruct((b, s, D_INNER), F32)
    scratch = [pltpu.VMEM((CHUNK, CONV_DIM), F32), pltpu.VMEM((SSM_HEADS // 2, D_STATE, LANES), F32)]
    if direction == 0:
        conv_w, conv_b, d_skip = conv
        in_specs = [
            pl.BlockSpec((1, SSD_HALO, CONV_DIM), lambda bi, c: (bi, jnp.maximum(c * hb - 1, 0), 0)),
            x_spec,
            pl.BlockSpec((1, SSD_HALO, CONV_DIM),
                         lambda bi, c: (bi, jnp.minimum((c + 1) * hb, n_hblocks - 1), 0)),
            pl.BlockSpec(conv_w.shape, const_map),
            pl.BlockSpec((1, CONV_DIM), const_map),
            pl.BlockSpec((1, D_INNER), const_map)] + small
        args = [x_in, x_in, x_in, conv_w, conv_b.reshape(1, CONV_DIM), d_skip, dt_raw, dt_bias, a_neg]
        out_specs = [y_spec, x_spec]
        out_shape = [y_shape, jax.ShapeDtypeStruct((b, s, CONV_DIM), BF16)]
        scratch = [pltpu.VMEM((SSD_EXT, CONV_DIM), F32)] + scratch
    else:
        in_specs = [x_spec, y_spec] + small
        args = [x_in, y_in, dt_raw, dt_bias, a_neg]
        out_specs = y_spec
        out_shape = y_shape
    return pl.pallas_call(
        functools.partial(_ssd_kernel, direction=direction, nc=nc),
        grid=(b, nc),
        in_specs=in_specs,
        out_specs=out_specs,
        out_shape=out_shape,
        scratch_shapes=scratch,
        compiler_params=_cparams(2),
        name="ssd_scan",
    )(*args)


A_PAIR_HEADS = tuple((2 * p, 2 * p + 1) for p in range(N_HEADS // 2))
B_PAIR_HEADS = tuple((8 * t + j, 8 * t + 4 + j) for t in range(2) for j in range(4))


def _prepare(p):
    aw = N_HEADS * HEAD_DIM
    scale = HEAD_DIM ** -0.5
    out = {}
    wa = p["a_w_qkv"][0].reshape(D_MODEL, N_DIL, 3, aw)
    wa = wa * jnp.asarray([scale, 1.0, 1.0], F32)[None, None, :, None]
    out["a_w_qkv"] = [wa[:, gi].reshape(D_MODEL, 3 * aw).astype(BF16) for gi in range(N_DIL)]
    out["a_w_o"] = p["a_w_o"][0].astype(BF16)
    perm = np.asarray(B_PAIR_HEADS).reshape(-1)
    cols = (perm[:, None] * HEAD_DIM + np.arange(HEAD_DIM)[None, :]).reshape(-1)
    wb = p["b_w_qkv"][0]
    out["b_w_qkv"] = jnp.concatenate([wb[:, :aw][:, cols] * scale, wb[:, aw:]], axis=1).astype(BF16)
    out["b_w_o"] = p["b_w_o"][0][cols, :].astype(BF16)
    out["b_sink"] = p["b_sink"][0].astype(F32)
    wc = p["c_w_in"][0].astype(BF16)
    out["c_w"] = (_chunk_cols(wc[:, :D_MODEL]), _chunk_cols(wc[:, D_MODEL:2 * D_MODEL]),
                  _chunk_cols(wc[:, 2 * D_MODEL:]), _chunk_cols(p["c_conv_w"][0].astype(F32)),
                  p["c_w_out"][0].astype(BF16).reshape(D_MODEL // MID_CHUNK, MID_CHUNK, D_MODEL))
    wd = p["d_w_in"][0]
    pad = jnp.zeros((D_MODEL, LANES - 2 * SSM_HEADS), F32)
    out["d_w_in"] = jnp.concatenate([wd, pad], axis=1).astype(BF16)
    lane_pad = jnp.zeros((LANES - 2 * SSM_HEADS,), F32)
    out["d_conv_w"] = p["d_conv_w"][0].astype(F32)
    out["d_conv_b"] = p["d_conv_b"][0].astype(F32)
    out["d_dt_bias"] = jnp.concatenate([p["d_dt_bias"][0].reshape(-1).astype(F32), lane_pad]).reshape(1, LANES)
    out["d_a_neg"] = jnp.concatenate([-jnp.exp(p["d_a_log"][0].reshape(-1).astype(F32)), lane_pad]).reshape(1, LANES)
    out["d_skip"] = jnp.repeat(p["d_skip"][0].astype(F32), HEAD_DIM).reshape(1, D_INNER)
    out["d_norm_g"] = p["d_norm_g"][0].astype(F32)
    out["d_w_out"] = p["d_w_out"][0].astype(BF16)
    ffn = []
    for i in range(DEPTH):
        wu = p["ffn_w_up"][i].astype(BF16)
        ffn.append((_chunk_cols(wu[:, :D_FF]), _chunk_cols(wu[:, D_FF:]),
                    _chunk_cols(p["ffn_conv_w"][i].astype(F32)),
                    _chunk_cols(p["ffn_conv_b"][i].astype(F32).reshape(1, D_FF)),
                    p["ffn_w_down"][i].astype(BF16).reshape(D_FF // MID_CHUNK, MID_CHUNK, D_MODEL)))
    out["ffn"] = ffn
    out["a_bias"] = [_attn_bias(p["rel_bias"], window // (2 * r), r, A_PAIR_HEADS)
                     for window, r in DILATIONS]
    out["b_bias"] = _attn_bias(p["rel_bias"], B_HALF_WINDOW, 1, B_PAIR_HEADS)
    out["norm_g"] = p["norm_g"].astype(F32)
    out["final_g"] = p["final_g"].astype(F32)
    return out


def _mixer_dilated(x2, g, w, bsz, seq):
    aw = N_HEADS * HEAD_DIM
    d = x2.shape[1]
    prev = None
    prev_r = 1
    for gi, (window, r) in enumerate(DILATIONS):
        length = seq // r
        xr = x2 if r == 1 else x2.reshape(bsz, length, r, d).transpose(0, 2, 1, 3).reshape(bsz * seq, d)
        (qkv,) = _norm_proj(xr, g, w["a_w_qkv"][gi], [(0, 3 * aw)], [BF16])
        view = qkv.reshape(bsz * r, length, 3 * aw)
        if prev is not None:
            k = r // prev_r

            def regroup(t):
                t = t.reshape(bsz, prev_r, length, k, t.shape[-1]).transpose(0, 3, 1, 2, 4)
                return t.reshape(bsz * r, length, t.shape[-1])

            prev = (regroup(prev[0]), regroup(prev[1]))
        res = _attention(
            view, view, view, qcol=lambda c: 0, kcol=lambda c: 1, vcol=lambda c: 2,
            kv_width=aw, bias=w["a_bias"][gi], length=length, stride=1, halo=window // (2 * r),
            pair_heads=A_PAIR_HEADS, kv_div=1, prev=prev, want_lse=True, out_dtype=F32)
        prev = tuple(res)
        prev_r = r
    o = prev[0].reshape(bsz, prev_r, seq // prev_r, aw).transpose(0, 2, 1, 3)
    return _out_proj(x2, o.reshape(bsz * seq, aw), w["a_w_o"])


def _mixer_window(x2, g, w, bsz, seq):
    aw = N_HEADS * HEAD_DIM
    kvw = B_KV_HEADS * HEAD_DIM
    n = aw + 2 * kvw
    (qkv,) = _norm_proj(x2, g, w["b_w_qkv"], [(0, n)], [BF16])
    view = qkv.reshape(bsz, seq, n)
    (o,) = _attention(view[:, :, :aw], view, view,
                      qcol=lambda c: 0, kcol=lambda c: aw // kvw, vcol=lambda c: aw // kvw + 1,
                      kv_width=kvw, bias=w["b_bias"], length=seq, stride=1, halo=B_HALF_WINDOW,
                      pair_heads=B_PAIR_HEADS, kv_div=4, sink=w["b_sink"], out_dtype=BF16)
    return _out_proj(x2, o.reshape(bsz * seq, aw), w["b_w_o"])


def _mixer_ssd(x2, g, w, bsz, seq):
    z, xbc, dt_raw = _norm_proj(
        x2, g, w["d_w_in"],
        [(0, D_INNER), (D_INNER, CONV_DIM), (D_INNER + CONV_DIM, LANES)], [BF16, BF16, F32])
    xbc = xbc.reshape(bsz, seq, CONV_DIM)
    dt_raw = dt_raw.reshape(bsz, seq, LANES)
    y, xc = _ssd(xbc, dt_raw, w["d_dt_bias"], w["d_a_neg"], 0,
                 conv=(w["d_conv_w"], w["d_conv_b"], w["d_skip"]))
    y = _ssd(xc, dt_raw, w["d_dt_bias"], w["d_a_neg"], 1, y_in=y)
    return _ssd_out(x2, y.reshape(bsz * seq, D_INNER), z, w["d_norm_g"], w["d_w_out"])


def _trunk(x, w):
    bsz, seq, d = x.shape
    x2 = x.reshape(bsz * seq, d)
    ng = w["norm_g"]
    for i in range(DEPTH):
        kind = i % 4
        if kind == 0:
            x2 = _mixer_dilated(x2, ng[i, 0], w, bsz, seq)
        elif kind == 1:
            x2 = _mixer_window(x2, ng[i, 0], w, bsz, seq)
        elif kind == 2:
            x2 = _gated_mlp(x2, ng[i, 0], w["c_w"], "sconv", seq)
        else:
            x2 = _mixer_ssd(x2, ng[i, 0], w, bsz, seq)
        x2 = _gated_mlp(x2, ng[i, 1], w["ffn"][i], "ffn", seq,
                        final_g=w["final_g"] if i == DEPTH - 1 else None)
    return x2.reshape(bsz, seq, d)


def kernel(x_prompt, x_sample, rel_bias, a_w_qkv, a_w_o, b_w_qkv, b_sink, b_w_o, c_w_in, c_conv_w,
           c_w_out, d_w_in, d_conv_w, d_conv_b, d_dt_bias, d_a_log, d_skip, d_norm_g, d_w_out,
           ffn_w_up, ffn_conv_w, ffn_conv_b, ffn_w_down, norm_g, final_g):
    w = _prepare(dict(
        rel_bias=rel_bias, a_w_qkv=a_w_qkv, a_w_o=a_w_o, b_w_qkv=b_w_qkv, b_sink=b_sink, b_w_o=b_w_o,
        c_w_in=c_w_in, c_conv_w=c_conv_w, c_w_out=c_w_out, d_w_in=d_w_in, d_conv_w=d_conv_w,
        d_conv_b=d_conv_b, d_dt_bias=d_dt_bias, d_a_log=d_a_log, d_skip=d_skip, d_norm_g=d_norm_g,
        d_w_out=d_w_out, ffn_w_up=ffn_w_up, ffn_conv_w=ffn_conv_w, ffn_conv_b=ffn_conv_b,
        ffn_w_down=ffn_w_down, norm_g=norm_g, final_g=final_g))
    return (_trunk(x_prompt, w), _trunk(x_sample, w))
```

```python
import functools
import math

import numpy as np
import jax
import jax.numpy as jnp
from jax import lax
from jax.experimental import pallas as pl
from jax.experimental.pallas import tpu as pltpu

F32 = jnp.float32
BF16 = jnp.bfloat16

D_MODEL = 1024
DEPTH = 4
HEAD_DIM = 64
N_HEADS = 16
DILATIONS = ((128, 1), (512, 4), (2048, 16))
N_DIL = 3
B_KV_HEADS = 4
B_HALF_WINDOW = 128
NUM_BUCKETS = 32
MAX_DISTANCE = 1024
D_INNER = 2 * D_MODEL
SSM_HEADS = 32
SSM_GROUPS = 4
D_STATE = 128
CHUNK = 128
CONV_DIM = D_INNER + 2 * SSM_GROUPS * D_STATE
D_FF = 2816
EPS = 1e-6
NEG_INF = -1e30

LANES = 128
BF16_ROWS = 16
VMEM_LIMIT = 56 * 1024 * 1024
ROW_TILE = 512
MID_CHUNK = 256
ATT_Q = 128


def _cparams(n_axes):
    return pltpu.CompilerParams(dimension_semantics=("arbitrary",) * n_axes,
                                vmem_limit_bytes=VMEM_LIMIT)


def _resident(shape):
    nd = len(shape)
    return pl.BlockSpec(shape, lambda *_: (0,) * nd, pipeline_mode=pl.Buffered(1))


def _rms(x, g):
    return x * lax.rsqrt(jnp.mean(x * x, axis=-1, keepdims=True) + EPS) * g


def _silu(t):
    return t / (1.0 + jnp.exp(-t))


def _norm_proj_kernel(x_ref, g_ref, w_ref, *rest, segs, col_chunk):
    o_refs, h_ref = rest[:-1], rest[-1]
    h_ref[...] = _rms(x_ref[...], g_ref[...]).astype(BF16)
    for o_ref, (start, width) in zip(o_refs, segs):
        for c in range(0, width, col_chunk):
            cw = min(col_chunk, width - c)
            o_ref[:, c:c + cw] = jnp.dot(h_ref[...], w_ref[:, start + c:start + c + cw],
                                         preferred_element_type=F32).astype(o_ref.dtype)


def _norm_proj(x2, g, w, segs, dtypes, tm=ROW_TILE):
    m, d = x2.shape
    n = w.shape[1]
    out_shape = [jax.ShapeDtypeStruct((m, wd), dt) for (_, wd), dt in zip(segs, dtypes)]
    return pl.pallas_call(
        functools.partial(_norm_proj_kernel, segs=tuple(segs), col_chunk=512),
        grid=(m // tm,),
        in_specs=[pl.BlockSpec((tm, d), lambda i: (i, 0)),
                  pl.BlockSpec((1, d), lambda i: (0, 0)),
                  _resident((d, n))],
        out_specs=[pl.BlockSpec((tm, wd), lambda i: (i, 0)) for (_, wd) in segs],
        out_shape=out_shape,
        scratch_shapes=[pltpu.VMEM((tm, d), BF16)],
        compiler_params=_cparams(1),
        name="norm_proj",
    )(x2, g.reshape(1, d), w)


HALO = BF16_ROWS


def _conv3(t, w, tm):
    n = tm + 2 * HALO
    up = pltpu.roll(t, 1, 0)[HALO:HALO + tm]
    dn = pltpu.roll(t, n - 1, 0)[HALO:HALO + tm]
    return w[0:1] * up + w[1:2] * t[HALO:HALO + tm] + w[2:3] * dn


def _gated_mlp_kernel(xp_ref, x_ref, xn_ref, g_ref, *rest, mode, tm, tiles_per_seq, nchunks,
                      final_norm):
    if mode == "ffn":
        wa_ref, wu_ref, cw_ref, cb_ref, wd_ref, fg_ref, o_ref, h_ref, acc_ref = rest
    else:
        wb_ref, wc_ref, wx_ref, cw_ref, wd_ref, fg_ref, o_ref, h_ref, acc_ref = rest
    pos = pl.program_id(0) % tiles_per_seq
    fprev = (pos != 0).astype(F32)
    fnext = (pos != tiles_per_seq - 1).astype(F32)
    g = g_ref[...]
    h_ref[0:HALO, :] = (_rms(xp_ref[...], g) * fprev).astype(BF16)
    h_ref[HALO:HALO + tm, :] = _rms(x_ref[...], g).astype(BF16)
    h_ref[HALO + tm:, :] = (_rms(xn_ref[...], g) * fnext).astype(BF16)
    acc_ref[...] = jnp.zeros_like(acc_ref)

    def body(c, carry):
        h_ext = h_ref[...]
        h_main = h_ref[HALO:HALO + tm, :]
        if mode == "ffn":
            a = jnp.dot(h_ext, wa_ref[c], preferred_element_type=F32)
            u = jnp.dot(h_main, wu_ref[c], preferred_element_type=F32)
            t = _conv3(a, cw_ref[c], tm) + cb_ref[c]
            gt = _silu(t) * u
        else:
            cg = jnp.dot(h_ext, wc_ref[c], preferred_element_type=F32)
            xi = jnp.dot(h_ext, wx_ref[c], preferred_element_type=F32)
            bg = jnp.dot(h_main, wb_ref[c], preferred_element_type=F32)
            gt = bg * _conv3(cg * xi, cw_ref[c], tm)
        acc_ref[...] += jnp.dot(gt.astype(BF16), wd_ref[c], preferred_element_type=F32)
        return carry

    lax.fori_loop(0, nchunks, body, 0, unroll=True)
    y = x_ref[...] + acc_ref[...]
    if final_norm:
        y = _rms(y, fg_ref[...])
    o_ref[...] = y


def _chunk_cols(w, cw=MID_CHUNK):
    k, n = w.shape
    return w.reshape(k, n // cw, cw).transpose(1, 0, 2)


def _gated_mlp(x2, g, weights, mode, seq, final_g=None, tm=ROW_TILE):
    m, d = x2.shape
    nchunks = weights[-1].shape[0]
    hb = tm // HALO
    n_hblocks = m // HALO
    in_specs = [
        pl.BlockSpec((HALO, d), lambda i: (jnp.maximum(i * hb - 1, 0), 0)),
        pl.BlockSpec((tm, d), lambda i: (i, 0)),
        pl.BlockSpec((HALO, d), lambda i: (jnp.minimum((i + 1) * hb, n_hblocks - 1), 0)),
        pl.BlockSpec((1, d), lambda i: (0, 0)),
    ] + [_resident(w.shape) for w in weights] + [pl.BlockSpec((1, d), lambda i: (0, 0))]
    fg = (final_g if final_g is not None else jnp.ones((d,), F32)).reshape(1, d)
    return pl.pallas_call(
        functools.partial(_gated_mlp_kernel, mode=mode, tm=tm, tiles_per_seq=seq // tm,
                          nchunks=nchunks, final_norm=final_g is not None),
        grid=(m // tm,),
        in_specs=in_specs,
        out_specs=pl.BlockSpec((tm, d), lambda i: (i, 0)),
        out_shape=jax.ShapeDtypeStruct((m, d), F32),
        scratch_shapes=[pltpu.VMEM((tm + 2 * HALO, d), BF16), pltpu.VMEM((tm, d), F32)],
        compiler_params=_cparams(1),
        name="gated_mlp_" + mode,
    )(x2, x2, x2, g.reshape(1, d), *weights, fg)


def _out_proj_kernel(x_ref, o_ref, w_ref, y_ref):
    y_ref[...] = x_ref[...] + jnp.dot(o_ref[...].astype(BF16), w_ref[...],
                                      preferred_element_type=F32)


def _out_proj(x2, o2, w, tm=ROW_TILE):
    m, d = x2.shape
    k = o2.shape[1]
    return pl.pallas_call(
        _out_proj_kernel,
        grid=(m // tm,),
        in_specs=[pl.BlockSpec((tm, d), lambda i: (i, 0)),
                  pl.BlockSpec((tm, k), lambda i: (i, 0)),
                  _resident((k, d))],
        out_specs=pl.BlockSpec((tm, d), lambda i: (i, 0)),
        out_shape=jax.ShapeDtypeStruct((m, d), F32),
        compiler_params=_cparams(1),
        name="out_proj",
    )(x2, o2, w)


def _ssd_out_kernel(x_ref, y_ref, z_ref, g_ref, w_ref, o_ref):
    yv = y_ref[...] * _silu(z_ref[...].astype(F32))
    hn = _rms(yv, g_ref[...]).astype(BF16)
    o_ref[...] = x_ref[...] + jnp.dot(hn, w_ref[...], preferred_element_type=F32)


def _ssd_out(x2, y2, z2, g, w, tm=ROW_TILE):
    m, d = x2.shape
    k = y2.shape[1]
    return pl.pallas_call(
        _ssd_out_kernel,
        grid=(m // tm,),
        in_specs=[pl.BlockSpec((tm, d), lambda i: (i, 0)),
                  pl.BlockSpec((tm, k), lambda i: (i, 0)),
                  pl.BlockSpec((tm, k), lambda i: (i, 0)),
                  pl.BlockSpec((1, k), lambda i: (0, 0)),
                  _resident((k, d))],
        out_specs=pl.BlockSpec((tm, d), lambda i: (i, 0)),
        out_shape=jax.ShapeDtypeStruct((m, d), F32),
        compiler_params=_cparams(1),
        name="ssd_out",
    )(x2, y2, z2, g.reshape(1, k), w)


def _attn_kernel(*refs, q_rows, halo, pair_heads, kv_div, has_sink, has_prev, want_lse, nb):
    refs = list(refs)
    q_ref, kp_ref, k_ref, kn_ref, vp_ref, v_ref, vn_ref, bias_ref = refs[:8]
    refs = refs[8:]
    sink_ref = refs.pop(0) if has_sink else None
    if has_prev:
        op_ref, lp_ref = refs.pop(0), refs.pop(0)
    o_ref = refs.pop(0)
    lse_ref = refs.pop(0) if want_lse else None
    kcat, vcat, pv_ref = refs
    q, h = q_rows, halo
    i = pl.program_id(2)
    variant = jnp.where(i == 0, 0, jnp.where(i == nb - 1, 2, 1))
    kcat[0:h] = kp_ref[0]
    kcat[h:h + q] = k_ref[0]
    kcat[h + q:] = kn_ref[0]
    vcat[0:h] = vp_ref[0]
    vcat[h:h + q] = v_ref[0]
    vcat[h + q:] = vn_ref[0]
    lane = lax.broadcasted_iota(jnp.int32, (q, LANES), 1)
    lo = lane < HEAD_DIM
    lo_b = (lax.broadcasted_iota(jnp.int32, (1, LANES), 1) < HEAD_DIM).astype(BF16)
    hi_b = 1.0 - lo_b
    m_tile = jnp.zeros((q, LANES), F32)
    l_tile = jnp.ones((q, LANES), F32)
    for p, heads in enumerate(pair_heads):
        cols = slice(p * LANES, (p + 1) * LANES)
        kcols = slice((p // kv_div) * LANES, (p // kv_div + 1) * LANES)
        q2 = q_ref[0, :, cols]
        qs = jnp.concatenate([q2 * lo_b, q2 * hi_b], axis=0)
        s = lax.dot_general(qs, kcat[:, kcols], (((1,), (1,)), ((), ())),
                            preferred_element_type=F32) + bias_ref[variant, p]
        m = jnp.max(s, axis=-1, keepdims=True)
        e = jnp.exp(s - m)
        l = jnp.sum(e, axis=-1, keepdims=True)
        pv_ref[p] = jnp.dot(e.astype(BF16), vcat[:, kcols], preferred_element_type=F32)
        h_lo, h_hi = heads
        m_tile = jnp.where(lane == h_lo, m[:q], jnp.where(lane == h_hi, m[q:], m_tile))
        l_tile = jnp.where(lane == h_lo, l[:q], jnp.where(lane == h_hi, l[q:], l_tile))
    if has_sink:
        sink_row = sink_ref[...]
        m_all = jnp.maximum(m_tile, sink_row)
        rescale = jnp.exp(m_tile - m_all)
        l_tile = l_tile * rescale + jnp.exp(sink_row - m_all)
        w_cur = rescale / l_tile
        m_tile = m_all
    else:
        w_cur = 1.0 / l_tile
    if want_lse or has_prev:
        lse = m_tile + jnp.log(l_tile)
    if has_prev:
        lp = lp_ref[0]
        n = jnp.maximum(lp, lse)
        n = n + jnp.log(jnp.exp(lp - n) + jnp.exp(lse - n))
        w_cur = w_cur * jnp.exp(lse - n)
        w_prev = jnp.exp(lp - n)
        lse = n
    if want_lse:
        lse_ref[0] = lse
    for p, (h_lo, h_hi) in enumerate(pair_heads):
        cols = slice(p * LANES, (p + 1) * LANES)
        pv = pv_ref[p]
        o = jnp.where(lo, pv[:q] * w_cur[:, h_lo:h_lo + 1], pv[q:] * w_cur[:, h_hi:h_hi + 1])
        if has_prev:
            o = o + op_ref[0, :, cols] * jnp.where(lo, w_prev[:, h_lo:h_lo + 1],
                                                   w_prev[:, h_hi:h_hi + 1])
        o_ref[0, :, cols] = o.astype(o_ref.dtype)


def _attention(q_arr, k_arr, v_arr, qcol, kcol, vcol, kv_width, bias, length, stride, halo,
               pair_heads, kv_div, sink=None, prev=None, want_lse=False, out_dtype=F32):
    b = q_arr.shape[0]
    q = ATT_Q
    nb = length // q
    assert nb >= 2 and length % q == 0 and q % halo == 0
    qh = q // halo
    n_hblocks = length // halo
    nk = q + 2 * halo
    width = N_HEADS * HEAD_DIM

    def main_map(col):
        return lambda bi, c, i: (bi, i, col(c))

    def prev_map(col):
        return lambda bi, c, i: (bi, jnp.maximum(i * qh - 1, 0), col(c))

    def next_map(col):
        return lambda bi, c, i: (bi, jnp.minimum((i + 1) * qh, n_hblocks - 1), col(c))

    in_specs = [pl.BlockSpec((1, q, width), main_map(qcol))]
    args = [q_arr]
    for arr, col in ((k_arr, kcol), (v_arr, vcol)):
        in_specs += [pl.BlockSpec((1, halo, kv_width), prev_map(col)),
                     pl.BlockSpec((1, q, kv_width), main_map(col)),
                     pl.BlockSpec((1, halo, kv_width), next_map(col))]
        args += [arr, arr, arr]
    in_specs.append(_resident(bias.shape))
    args.append(bias)
    if sink is not None:
        in_specs.append(pl.BlockSpec((1, LANES), lambda bi, c, i: (0, 0)))
        args.append(sink)
    if prev is not None:
        in_specs += [pl.BlockSpec((1, q, width), main_map(lambda c: c)),
                     pl.BlockSpec((1, q, LANES), main_map(lambda c: c))]
        args += list(prev)
    out_specs = [pl.BlockSpec((1, q, width), main_map(lambda c: c))]
    out_shape = [jax.ShapeDtypeStruct((b, length, stride * width), out_dtype)]
    if want_lse:
        out_specs.append(pl.BlockSpec((1, q, LANES), main_map(lambda c: c)))
        out_shape.append(jax.ShapeDtypeStruct((b, length, stride * LANES), F32))
    return pl.pallas_call(
        functools.partial(_attn_kernel, q_rows=q, halo=halo, pair_heads=tuple(pair_heads),
                          kv_div=kv_div, has_sink=sink is not None, has_prev=prev is not None,
                          want_lse=want_lse, nb=nb),
        grid=(b, stride, nb),
        in_specs=in_specs,
        out_specs=out_specs,
        out_shape=out_shape,
        scratch_shapes=[pltpu.VMEM((nk, kv_width), BF16), pltpu.VMEM((nk, kv_width), BF16),
                        pltpu.VMEM((len(pair_heads), 2 * q, LANES), F32)],
        compiler_params=_cparams(3),
        name="banded_attention",
    )(*args)


def _t5_bucket(rel):
    half = NUM_BUCKETS // 2
    exact = half // 2
    n = np.abs(rel)
    log_ratio = np.log(np.maximum(n, 1) / exact) / math.log(MAX_DISTANCE / exact)
    large = np.minimum(exact + (log_ratio * (half - exact)).astype(np.int64), half - 1)
    return np.where(rel > 0, half, 0) + np.where(n < exact, n, large)


def _attn_bias(rel_bias, halo, stride, pair_heads):
    q = ATT_Q
    nk = q + 2 * halo
    i = np.arange(q)[:, None]
    j = np.arange(nk)[None, :]
    rel = (j - halo) - i
    band = np.abs(rel) <= halo
    heads = np.asarray(pair_heads).reshape(-1)
    onehot = (_t5_bucket(rel * stride)[None] == np.arange(NUM_BUCKETS)[:, None, None])
    table = jnp.einsum("hb,bqk->hqk", rel_bias.astype(F32).T[heads], jnp.asarray(onehot).astype(F32),
                       precision=lax.Precision.HIGHEST).reshape(len(pair_heads), 2 * q, nk)
    valid = np.stack([band & (j >= halo), band, band & (j < halo + q)])
    valid = np.concatenate([valid, valid], axis=1)[:, None]
    return jnp.where(valid, table[None], NEG_INF)


SSD_HALO = BF16_ROWS
SSD_EXT = CHUNK + 2 * SSD_HALO


def _ssd_kernel(*refs, direction, nc):
    d = direction
    if d == 0:
        (xp_ref, x_ref, xn_ref, cw_ref, cb_ref, dsk_ref, dt_ref, dtb_ref, a_ref,
         y_ref, xo_ref, xe_ref, xc_ref, st_ref) = refs
    else:
        xi_ref, yin_ref, dt_ref, dtb_ref, a_ref, y_ref, xc_ref, st_ref = refs
    c = pl.program_id(1)

    @pl.when(c == 0)
    def _():
        st_ref[...] = jnp.zeros_like(st_ref)

    if d == 0:
        fprev = (c != 0).astype(F32)
        fnext = (c != nc - 1).astype(F32)
        h0 = SSD_HALO
        xe_ref[0:h0] = xp_ref[0].astype(F32) * fprev
        xe_ref[h0:h0 + CHUNK] = x_ref[0].astype(F32)
        xe_ref[h0 + CHUNK:] = xn_ref[0].astype(F32) * fnext
        cchunk = 512
        for j in range(0, CONV_DIM, cchunk):
            xe = xe_ref[:, j:j + cchunk]
            w = cw_ref[:, j:j + cchunk]
            t = (w[0:1] * pltpu.roll(xe, 2, 0)[h0:h0 + CHUNK]
                 + w[1:2] * pltpu.roll(xe, 1, 0)[h0:h0 + CHUNK]
                 + w[2:3] * xe[h0:h0 + CHUNK]
                 + w[3:4] * pltpu.roll(xe, SSD_EXT - 1, 0)[h0:h0 + CHUNK]) + cb_ref[:, j:j + cchunk]
            xc = _silu(t)
            xc_ref[:, j:j + cchunk] = xc
            xo_ref[0, :, j:j + cchunk] = xc.astype(xo_ref.dtype)
    else:
        xc_ref[...] = xi_ref[0].astype(F32)

    dtr = dt_ref[0] + dtb_ref[...]
    dt = jnp.maximum(dtr, 0.0) + jnp.log1p(jnp.exp(-jnp.abs(dtr)))
    adt = dt * a_ref[...]
    dt_t = dt.T
    adt_t = adt.T
    li = lax.broadcasted_iota(jnp.int32, (CHUNK, CHUNK), 0)
    ui = lax.broadcasted_iota(jnp.int32, (CHUNK, CHUNK), 1)
    mask = (ui <= li) if d == 0 else (ui >= li)
    tri = mask.astype(F32)
    hp = lax.Precision.HIGHEST
    acs = jnp.dot(tri, adt, precision=hp, preferred_element_type=F32)
    acs_t = lax.dot_general(adt_t, tri, (((1,), (1,)), ((), ())), precision=hp,
                            preferred_element_type=F32)
    tot = jnp.sum(adt_t, axis=1, keepdims=True)
    lo = lax.broadcasted_iota(jnp.int32, (CHUNK, LANES), 1) < HEAD_DIM

    heads_per_group = SSM_HEADS // SSM_GROUPS
    pairs_per_group = heads_per_group // 2
    for grp in range(SSM_GROUPS):
        bcol = D_INNER + grp * D_STATE
        ccol = D_INNER + SSM_GROUPS * D_STATE + grp * D_STATE
        bg = xc_ref[:, bcol:bcol + D_STATE]
        cg = xc_ref[:, ccol:ccol + D_STATE].astype(BF16)
        cb = lax.dot_general(cg, bg.astype(BF16), (((1,), (1,)), ((), ())),
                             preferred_element_type=F32)
        bg_t = bg.T
        for pp in range(pairs_per_group):
            p = grp * pairs_per_group + pp
            cols = slice(p * LANES, (p + 1) * LANES)
            x2 = xc_ref[:, cols]
            x2b = x2.astype(BF16)
            zero = jnp.zeros_like(x2b)
            xbd = jnp.concatenate([jnp.where(lo, x2b, zero), jnp.where(lo, zero, x2b)], axis=0)
            ms, bts, es, cds = [], [], [], []
            for hh in (d * SSM_HEADS + 2 * p, d * SSM_HEADS + 2 * p + 1):
                a_col = jnp.broadcast_to(acs[:, hh:hh + 1], (CHUNK, CHUNK))
                a_row = acs_t[hh:hh + 1, :]
                dt_row = dt_t[hh:hh + 1, :]
                lmat = jnp.exp(jnp.where(mask, a_col - a_row, NEG_INF))
                ms.append((cb * lmat * dt_row).astype(BF16))
                w_row = jnp.exp(tot[hh:hh + 1, :] - a_row) * dt_row
                bts.append((bg_t * w_row).astype(BF16))
                es.append(jnp.exp(a_col))
                cds.append(jnp.exp(tot[hh:hh + 1, :]))
            st = st_ref[p]
            y = jnp.dot(jnp.concatenate(ms, axis=1), xbd, preferred_element_type=F32)
            y = y + jnp.dot(cg, st.astype(BF16), preferred_element_type=F32) * jnp.where(lo, es[0], es[1])
            if d == 0:
                y = y + x2 * dsk_ref[:, cols]
            else:
                y = y + yin_ref[0, :, cols]
            y_ref[0, :, cols] = y
            st_ref[p] = (st * jnp.where(lo, cds[0], cds[1])
                         + jnp.dot(jnp.concatenate(bts, axis=1), xbd, preferred_element_type=F32))


def _ssd(x_in, dt_raw, dt_bias, a_neg, direction, conv=None, y_in=None):
    b, s, _ = x_in.shape
    nc = s // CHUNK
    hb = CHUNK // SSD_HALO
    n_hblocks = s // SSD_HALO

    def chunk_map(bi, c):
        return (bi, c if direction == 0 else nc - 1 - c, 0)

    def const_map(bi, c):
        return (0, 0)

    small = [pl.BlockSpec((1, CHUNK, LANES), chunk_map),
             pl.BlockSpec((1, LANES), const_map), pl.BlockSpec((1, LANES), const_map)]
    y_spec = pl.BlockSpec((1, CHUNK, D_INNER), chunk_map)
    x_spec = pl.BlockSpec((1, CHUNK, CONV_DIM), chunk_map)
    y_shape = jax.ShapeDtypeStruct((b, s, D_INNER), F32)
    scratch = [pltpu.VMEM((CHUNK, CONV_DIM), F32), pltpu.VMEM((SSM_HEADS // 2, D_STATE, LANES), F32)]
    if direction == 0:
        conv_w, conv_b, d_skip = conv
        in_specs = [
            pl.BlockSpec((1, SSD_HALO, CONV_DIM), lambda bi, c: (bi, jnp.maximum(c * hb - 1, 0), 0)),
            x_spec,
            pl.BlockSpec((1, SSD_HALO, CONV_DIM),
                         lambda bi, c: (bi, jnp.minimum((c + 1) * hb, n_hblocks - 1), 0)),
            pl.BlockSpec(conv_w.shape, const_map),
            pl.BlockSpec((1, CONV_DIM), const_map),
            pl.BlockSpec((1, D_INNER), const_map)] + small
        args = [x_in, x_in, x_in, conv_w, conv_b.reshape(1, CONV_DIM), d_skip, dt_raw, dt_bias, a_neg]
        out_specs = [y_spec, x_spec]
        out_shape = [y_shape, jax.ShapeDtypeStruct((b, s, CONV_DIM), BF16)]
        scratch = [pltpu.VMEM((SSD_EXT, CONV_DIM), F32)] + scratch
    else:
        in_specs = [x_spec, y_spec] + small
        args = [x_in, y_in, dt_raw, dt_bias, a_neg]
        out_specs = y_spec
        out_shape = y_shape
    return pl.pallas_call(
        functools.partial(_ssd_kernel, direction=direction, nc=nc),
        grid=(b, nc),
        in_specs=in_specs,
        out_specs=out_specs,
        out_shape=out_shape,
        scratch_shapes=scratch,
        compiler_params=_cparams(2),
        name="ssd_scan",
    )(*args)


A_PAIR_HEADS = tuple((2 * p, 2 * p + 1) for p in range(N_HEADS // 2))
B_PAIR_HEADS = tuple((8 * t + j, 8 * t + 4 + j) for t in range(2) for j in range(4))


def _prepare(p):
    aw = N_HEADS * HEAD_DIM
    scale = HEAD_DIM ** -0.5
    out = {}
    wa = p["a_w_qkv"][0].reshape(D_MODEL, N_DIL, 3, aw)
    wa = wa * jnp.asarray([scale, 1.0, 1.0], F32)[None, None, :, None]
    out["a_w_qkv"] = [wa[:, gi].reshape(D_MODEL, 3 * aw).astype(BF16) for gi in range(N_DIL)]
    out["a_w_o"] = p["a_w_o"][0].astype(BF16)
    perm = np.asarray(B_PAIR_HEADS).reshape(-1)
    cols = (perm[:, None] * HEAD_DIM + np.arange(HEAD_DIM)[None, :]).reshape(-1)
    wb = p["b_w_qkv"][0]
    out["b_w_qkv"] = jnp.concatenate([wb[:, :aw][:, cols] * scale, wb[:, aw:]], axis=1).astype(BF16)
    out["b_w_o"] = p["b_w_o"][0][cols, :].astype(BF16)
    out["b_sink"] = jnp.concatenate([p["b_sink"][0].astype(F32),
                                     jnp.zeros((LANES - N_HEADS,), F32)]).reshape(1, LANES)
    wc = p["c_w_in"][0].astype(BF16)
    out["c_w"] = (_chunk_cols(wc[:, :D_MODEL]), _chunk_cols(wc[:, D_MODEL:2 * D_MODEL]),
                  _chunk_cols(wc[:, 2 * D_MODEL:]), _chunk_cols(p["c_conv_w"][0].astype(F32)),
                  p["c_w_out"][0].astype(BF16).reshape(D_MODEL // MID_CHUNK, MID_CHUNK, D_MODEL))
    wd = p["d_w_in"][0]
    pad = jnp.zeros((D_MODEL, LANES - 2 * SSM_HEADS), F32)
    out["d_w_in"] = jnp.concatenate([wd, pad], axis=1).astype(BF16)
    lane_pad = jnp.zeros((LANES - 2 * SSM_HEADS,), F32)
    out["d_conv_w"] = p["d_conv_w"][0].astype(F32)
    out["d_conv_b"] = p["d_conv_b"][0].astype(F32)
    out["d_dt_bias"] = jnp.concatenate([p["d_dt_bias"][0].reshape(-1).astype(F32), lane_pad]).reshape(1, LANES)
    out["d_a_neg"] = jnp.concatenate([-jnp.exp(p["d_a_log"][0].reshape(-1).astype(F32)), lane_pad]).reshape(1, LANES)
    out["d_skip"] = jnp.repeat(p["d_skip"][0].astype(F32), HEAD_DIM).reshape(1, D_INNER)
    out["d_norm_g"] = p["d_norm_g"][0].astype(F32)
    out["d_w_out"] = p["d_w_out"][0].astype(BF16)
    ffn = []
    for i in range(DEPTH):
        wu = p["ffn_w_up"][i].astype(BF16)
        ffn.append((_chunk_cols(wu[:, :D_FF]), _chunk_cols(wu[:, D_FF:]),
                    _chunk_cols(p["ffn_conv_w"][i].astype(F32)),
                    _chunk_cols(p["ffn_conv_b"][i].astype(F32).reshape(1, D_FF)),
                    p["ffn_w_down"][i].astype(BF16).reshape(D_FF // MID_CHUNK, MID_CHUNK, D_MODEL)))
    out["ffn"] = ffn
    out["a_bias"] = [_attn_bias(p["rel_bias"], window // (2 * r), r, A_PAIR_HEADS)
                     for window, r in DILATIONS]
    out["b_bias"] = _attn_bias(p["rel_bias"], B_HALF_WINDOW, 1, B_PAIR_HEADS)
    out["norm_g"] = p["norm_g"].astype(F32)
    out["final_g"] = p["final_g"].astype(F32)
    return out


def _mixer_dilated(x2, g, w, bsz, seq):
    aw = N_HEADS * HEAD_DIM
    d = x2.shape[1]
    prev = None
    prev_r = 1
    for gi, (window, r) in enumerate(DILATIONS):
        length = seq // r
        xr = x2 if r == 1 else x2.reshape(bsz, length, r, d).transpose(0, 2, 1, 3).reshape(bsz * seq, d)
        (qkv,) = _norm_proj(xr, g, w["a_w_qkv"][gi], [(0, 3 * aw)], [BF16])
        view = qkv.reshape(bsz * r, length, 3 * aw)
        if prev is not None:
            k = r // prev_r

            def regroup(t):
                t = t.reshape(bsz, prev_r, length, k, t.shape[-1]).transpose(0, 3, 1, 2, 4)
                return t.reshape(bsz * r, length, t.shape[-1])

            prev = (regroup(prev[0]), regroup(prev[1]))
        res = _attention(
            view, view, view, qcol=lambda c: 0, kcol=lambda c: 1, vcol=lambda c: 2,
            kv_width=aw, bias=w["a_bias"][gi], length=length, stride=1, halo=window // (2 * r),
            pair_heads=A_PAIR_HEADS, kv_div=1, prev=prev, want_lse=True, out_dtype=F32)
        prev = tuple(res)
        prev_r = r
    o = prev[0].reshape(bsz, prev_r, seq // prev_r, aw).transpose(0, 2, 1, 3)
    return _out_proj(x2, o.reshape(bsz * seq, aw), w["a_w_o"])


def _mixer_window(x2, g, w, bsz, seq):
    aw = N_HEADS * HEAD_DIM
    kvw = B_KV_HEADS * HEAD_DIM
    n = aw + 2 * kvw
    (qkv,) = _norm_proj(x2, g, w["b_w_qkv"], [(0, n)], [BF16])
    view = qkv.reshape(bsz, seq, n)
    (o,) = _attention(view[:, :, :aw], view, view,
                      qcol=lambda c: 0, kcol=lambda c: aw // kvw, vcol=lambda c: aw // kvw + 1,
                      kv_width=kvw, bias=w["b_bias"], length=seq, stride=1, halo=B_HALF_WINDOW,
                      pair_heads=B_PAIR_HEADS, kv_div=4, sink=w["b_sink"], out_dtype=BF16)
    return _out_proj(x2, o.reshape(bsz * seq, aw), w["b_w_o"])


def _mixer_ssd(x2, g, w, bsz, seq):
    z, xbc, dt_raw = _norm_proj(
        x2, g, w["d_w_in"],
        [(0, D_INNER), (D_INNER, CONV_DIM), (D_INNER + CONV_DIM, LANES)], [BF16, BF16, F32])
    xbc = xbc.reshape(bsz, seq, CONV_DIM)
    dt_raw = dt_raw.reshape(bsz, seq, LANES)
    y, xc = _ssd(xbc, dt_raw, w["d_dt_bias"], w["d_a_neg"], 0,
                 conv=(w["d_conv_w"], w["d_conv_b"], w["d_skip"]))
    y = _ssd(xc, dt_raw, w["d_dt_bias"], w["d_a_neg"], 1, y_in=y)
    return _ssd_out(x2, y.reshape(bsz * seq, D_INNER), z, w["d_norm_g"], w["d_w_out"])


def _trunk(x, w):
    bsz, seq, d = x.shape
    x2 = x.reshape(bsz * seq, d)
    ng = w["norm_g"]
    for i in range(DEPTH):
        kind = i % 4
        if kind == 0:
            x2 = _mixer_dilated(x2, ng[i, 0], w, bsz, seq)
        elif kind == 1:
            x2 = _mixer_window(x2, ng[i, 0], w, bsz, seq)
        elif kind == 2:
            x2 = _gated_mlp(x2, ng[i, 0], w["c_w"], "sconv", seq)
        else:
            x2 = _mixer_ssd(x2, ng[i, 0], w, bsz, seq)
        x2 = _gated_mlp(x2, ng[i, 1], w["ffn"][i], "ffn", seq,
                        final_g=w["final_g"] if i == DEPTH - 1 else None)
    return x2.reshape(bsz, seq, d)


def kernel(x_prompt, x_sample, rel_bias, a_w_qkv, a_w_o, b_w_qkv, b_sink, b_w_o, c_w_in, c_conv_w,
           c_w_out, d_w_in, d_conv_w, d_conv_b, d_dt_bias, d_a_log, d_skip, d_norm_g, d_w_out,
           ffn_w_up, ffn_conv_w, ffn_conv_b, ffn_w_down, norm_g, final_g):
    w = _prepare(dict(
        rel_bias=rel_bias, a_w_qkv=a_w_qkv, a_w_o=a_w_o, b_w_qkv=b_w_qkv, b_sink=b_sink, b_w_o=b_w_o,
        c_w_in=c_w_in, c_conv_w=c_conv_w, c_w_out=c_w_out, d_w_in=d_w_in, d_conv_w=d_conv_w,
        d_conv_b=d_conv_b, d_dt_bias=d_dt_bias, d_a_log=d_a_log, d_skip=d_skip, d_norm_g=d_norm_g,
        d_w_out=d_w_out, ffn_w_up=ffn_w_up, ffn_conv_w=ffn_conv_w, ffn_conv_b=ffn_conv_b,
        ffn_w_down=ffn_w_down, norm_g=norm_g, final_g=final_g))
    return (_trunk(x_prompt, w), _trunk(x_sample, w))
```

```python
import functools
import math

import numpy as np
import jax
import jax.numpy as jnp
from jax import lax
from jax.experimental import pallas as pl
from jax.experimental.pallas import tpu as pltpu

F32 = jnp.float32
BF16 = jnp.bfloat16

D_MODEL = 1024
DEPTH = 4
HEAD_DIM = 64
N_HEADS = 16
DILATIONS = ((128, 1), (512, 4), (2048, 16))
N_DIL = 3
B_KV_HEADS = 4
B_HALF_WINDOW = 128
NUM_BUCKETS = 32
MAX_DISTANCE = 1024
D_INNER = 2 * D_MODEL
SSM_HEADS = 32
SSM_GROUPS = 4
D_STATE = 128
CHUNK = 128
CONV_DIM = D_INNER + 2 * SSM_GROUPS * D_STATE
D_FF = 2816
EPS = 1e-6
NEG_INF = -1e30
LOG2E = 1.0 / math.log(2.0)

LANES = 128
BF16_ROWS = 16
VMEM_LIMIT = 56 * 1024 * 1024
ROW_TILE = 512
MID_CHUNK = 256
ATT_Q = 128


def _cparams(n_axes):
    return pltpu.CompilerParams(dimension_semantics=("arbitrary",) * n_axes,
                                vmem_limit_bytes=VMEM_LIMIT)


def _resident(shape):
    nd = len(shape)
    return pl.BlockSpec(shape, lambda *_: (0,) * nd, pipeline_mode=pl.Buffered(1))


def _rms(x, g):
    return x * lax.rsqrt(jnp.mean(x * x, axis=-1, keepdims=True) + EPS) * g


def _silu(t):
    return t / (1.0 + jnp.exp(-t))


def _norm_proj_kernel(x_ref, g_ref, w_ref, *rest, segs, col_chunk):
    o_refs, h_ref = rest[:-1], rest[-1]
    h_ref[...] = _rms(x_ref[...], g_ref[...]).astype(BF16)
    for o_ref, (start, width) in zip(o_refs, segs):
        for c in range(0, width, col_chunk):
            cw = min(col_chunk, width - c)
            o_ref[:, c:c + cw] = jnp.dot(h_ref[...], w_ref[:, start + c:start + c + cw],
                                         preferred_element_type=F32).astype(o_ref.dtype)


def _norm_proj(x2, g, w, segs, dtypes, tm=ROW_TILE):
    m, d = x2.shape
    n = w.shape[1]
    out_shape = [jax.ShapeDtypeStruct((m, wd), dt) for (_, wd), dt in zip(segs, dtypes)]
    return pl.pallas_call(
        functools.partial(_norm_proj_kernel, segs=tuple(segs), col_chunk=512),
        grid=(m // tm,),
        in_specs=[pl.BlockSpec((tm, d), lambda i: (i, 0)),
                  pl.BlockSpec((1, d), lambda i: (0, 0)),
                  _resident((d, n))],
        out_specs=[pl.BlockSpec((tm, wd), lambda i: (i, 0)) for (_, wd) in segs],
        out_shape=out_shape,
        scratch_shapes=[pltpu.VMEM((tm, d), BF16)],
        compiler_params=_cparams(1),
        name="norm_proj",
    )(x2, g.reshape(1, d), w)


HALO = BF16_ROWS


def _conv3(t, w, tm):
    n = tm + 2 * HALO
    up = pltpu.roll(t, 1, 0)[HALO:HALO + tm]
    dn = pltpu.roll(t, n - 1, 0)[HALO:HALO + tm]
    return w[0:1] * up + w[1:2] * t[HALO:HALO + tm] + w[2:3] * dn


def _gated_mlp_kernel(xp_ref, x_ref, xn_ref, g_ref, *rest, mode, tm, tiles_per_seq, nchunks,
                      final_norm):
    if mode == "ffn":
        wa_ref, wu_ref, cw_ref, cb_ref, wd_ref, fg_ref, o_ref, h_ref, acc_ref = rest
    else:
        wb_ref, wc_ref, wx_ref, cw_ref, wd_ref, fg_ref, o_ref, h_ref, acc_ref = rest
    pos = pl.program_id(0) % tiles_per_seq
    fprev = (pos != 0).astype(F32)
    fnext = (pos != tiles_per_seq - 1).astype(F32)
    g = g_ref[...]
    h_ref[0:HALO, :] = (_rms(xp_ref[...], g) * fprev).astype(BF16)
    h_ref[HALO:HALO + tm, :] = _rms(x_ref[...], g).astype(BF16)
    h_ref[HALO + tm:, :] = (_rms(xn_ref[...], g) * fnext).astype(BF16)
    acc_ref[...] = jnp.zeros_like(acc_ref)

    def body(c, carry):
        h_ext = h_ref[...]
        h_main = h_ref[HALO:HALO + tm, :]
        if mode == "ffn":
            a = jnp.dot(h_ext, wa_ref[c], preferred_element_type=F32)
            u = jnp.dot(h_main, wu_ref[c], preferred_element_type=F32)
            t = _conv3(a, cw_ref[c], tm) + cb_ref[c]
            gt = _silu(t) * u
        else:
            cg = jnp.dot(h_ext, wc_ref[c], preferred_element_type=F32)
            xi = jnp.dot(h_ext, wx_ref[c], preferred_element_type=F32)
            bg = jnp.dot(h_main, wb_ref[c], preferred_element_type=F32)
            gt = bg * _conv3(cg * xi, cw_ref[c], tm)
        acc_ref[...] += jnp.dot(gt.astype(BF16), wd_ref[c], preferred_element_type=F32)
        return carry

    lax.fori_loop(0, nchunks, body, 0, unroll=True)
    y = x_ref[...] + acc_ref[...]
    if final_norm:
        y = _rms(y, fg_ref[...])
    o_ref[...] = y


def _chunk_cols(w, cw=MID_CHUNK):
    k, n = w.shape
    return w.reshape(k, n // cw, cw).transpose(1, 0, 2)


def _gated_mlp(x2, g, weights, mode, seq, final_g=None, tm=ROW_TILE):
    m, d = x2.shape
    nchunks = weights[-1].shape[0]
    hb = tm // HALO
    n_hblocks = m // HALO
    in_specs = [
        pl.BlockSpec((HALO, d), lambda i: (jnp.maximum(i * hb - 1, 0), 0)),
        pl.BlockSpec((tm, d), lambda i: (i, 0)),
        pl.BlockSpec((HALO, d), lambda i: (jnp.minimum((i + 1) * hb, n_hblocks - 1), 0)),
        pl.BlockSpec((1, d), lambda i: (0, 0)),
    ] + [_resident(w.shape) for w in weights] + [pl.BlockSpec((1, d), lambda i: (0, 0))]
    fg = (final_g if final_g is not None else jnp.ones((d,), F32)).reshape(1, d)
    return pl.pallas_call(
        functools.partial(_gated_mlp_kernel, mode=mode, tm=tm, tiles_per_seq=seq // tm,
                          nchunks=nchunks, final_norm=final_g is not None),
        grid=(m // tm,),
        in_specs=in_specs,
        out_specs=pl.BlockSpec((tm, d), lambda i: (i, 0)),
        out_shape=jax.ShapeDtypeStruct((m, d), F32),
        scratch_shapes=[pltpu.VMEM((tm + 2 * HALO, d), BF16), pltpu.VMEM((tm, d), F32)],
        compiler_params=_cparams(1),
        name="gated_mlp_" + mode,
    )(x2, x2, x2, g.reshape(1, d), *weights, fg)


def _out_proj_kernel(x_ref, o_ref, w_ref, y_ref):
    y_ref[...] = x_ref[...] + jnp.dot(o_ref[...].astype(BF16), w_ref[...],
                                      preferred_element_type=F32)


def _out_proj(x2, o2, w, tm=ROW_TILE):
    m, d = x2.shape
    k = o2.shape[1]
    return pl.pallas_call(
        _out_proj_kernel,
        grid=(m // tm,),
        in_specs=[pl.BlockSpec((tm, d), lambda i: (i, 0)),
                  pl.BlockSpec((tm, k), lambda i: (i, 0)),
                  _resident((k, d))],
        out_specs=pl.BlockSpec((tm, d), lambda i: (i, 0)),
        out_shape=jax.ShapeDtypeStruct((m, d), F32),
        compiler_params=_cparams(1),
        name="out_proj",
    )(x2, o2, w)


def _ssd_out_kernel(x_ref, y_ref, z_ref, g_ref, w_ref, o_ref):
    yv = y_ref[...] * _silu(z_ref[...].astype(F32))
    hn = _rms(yv, g_ref[...]).astype(BF16)
    o_ref[...] = x_ref[...] + jnp.dot(hn, w_ref[...], preferred_element_type=F32)


def _ssd_out(x2, y2, z2, g, w, tm=ROW_TILE):
    m, d = x2.shape
    k = y2.shape[1]
    return pl.pallas_call(
        _ssd_out_kernel,
        grid=(m // tm,),
        in_specs=[pl.BlockSpec((tm, d), lambda i: (i, 0)),
                  pl.BlockSpec((tm, k), lambda i: (i, 0)),
                  pl.BlockSpec((tm, k), lambda i: (i, 0)),
                  pl.BlockSpec((1, k), lambda i: (0, 0)),
                  _resident((k, d))],
        out_specs=pl.BlockSpec((tm, d), lambda i: (i, 0)),
        out_shape=jax.ShapeDtypeStruct((m, d), F32),
        compiler_params=_cparams(1),
        name="ssd_out",
    )(x2, y2, z2, g.reshape(1, k), w)


def _attn_kernel(*refs, q_rows, halo, pair_heads, kv_div, has_sink, has_prev, want_lse, nb):
    refs = list(refs)
    q_ref, kp_ref, k_ref, kn_ref, vp_ref, v_ref, vn_ref, bias_ref = refs[:8]
    refs = refs[8:]
    sink_ref = refs.pop(0) if has_sink else None
    if has_prev:
        op_ref, lp_ref = refs.pop(0), refs.pop(0)
    o_ref = refs.pop(0)
    lse_ref = refs.pop(0) if want_lse else None
    kcat, vcat, pv_ref = refs
    q, h = q_rows, halo
    i = pl.program_id(2)
    variant = jnp.where(i == 0, 0, jnp.where(i == nb - 1, 2, 1))
    kcat[0:h] = kp_ref[0]
    kcat[h:h + q] = k_ref[0]
    kcat[h + q:] = kn_ref[0]
    vcat[0:h] = vp_ref[0]
    vcat[h:h + q] = v_ref[0]
    vcat[h + q:] = vn_ref[0]
    lane = lax.broadcasted_iota(jnp.int32, (q, LANES), 1)
    lo = lane < HEAD_DIM
    lo_b = (lax.broadcasted_iota(jnp.int32, (1, LANES), 1) < HEAD_DIM).astype(BF16)
    hi_b = 1.0 - lo_b
    m_tile = jnp.zeros((q, LANES), F32)
    l_tile = jnp.ones((q, LANES), F32)
    for p, heads in enumerate(pair_heads):
        cols = slice(p * LANES, (p + 1) * LANES)
        kcols = slice((p // kv_div) * LANES, (p // kv_div + 1) * LANES)
        q2 = q_ref[0, :, cols]
        qs = jnp.concatenate([q2 * lo_b, q2 * hi_b], axis=0)
        s = lax.dot_general(qs, kcat[:, kcols], (((1,), (1,)), ((), ())),
                            preferred_element_type=F32) + bias_ref[variant, p]
        m = jnp.max(s, axis=-1, keepdims=True)
        e = jnp.exp(s - m)
        l = jnp.sum(e, axis=-1, keepdims=True)
        pv_ref[p] = jnp.dot(e.astype(BF16), vcat[:, kcols], preferred_element_type=F32)
        h_lo, h_hi = heads
        m_tile = jnp.where(lane == h_lo, m[:q], jnp.where(lane == h_hi, m[q:], m_tile))
        l_tile = jnp.where(lane == h_lo, l[:q], jnp.where(lane == h_hi, l[q:], l_tile))
    if has_sink:
        sink_row = sink_ref[...]
        m_all = jnp.maximum(m_tile, sink_row)
        rescale = jnp.exp(m_tile - m_all)
        l_tile = l_tile * rescale + jnp.exp(sink_row - m_all)
        w_cur = rescale / l_tile
        m_tile = m_all
    else:
        w_cur = 1.0 / l_tile
    if want_lse or has_prev:
        lse = m_tile + jnp.log(l_tile)
    if has_prev:
        lp = lp_ref[0]
        n = jnp.maximum(lp, lse)
        n = n + jnp.log(jnp.exp(lp - n) + jnp.exp(lse - n))
        w_cur = w_cur * jnp.exp(lse - n)
        w_prev = jnp.exp(lp - n)
        lse = n
    if want_lse:
        lse_ref[0] = lse
    for p, (h_lo, h_hi) in enumerate(pair_heads):
        cols = slice(p * LANES, (p + 1) * LANES)
        pv = pv_ref[p]
        o = jnp.where(lo, pv[:q] * w_cur[:, h_lo:h_lo + 1], pv[q:] * w_cur[:, h_hi:h_hi + 1])
        if has_prev:
            o = o + op_ref[0, :, cols] * jnp.where(lo, w_prev[:, h_lo:h_lo + 1],
                                                   w_prev[:, h_hi:h_hi + 1])
        o_ref[0, :, cols] = o.astype(o_ref.dtype)


def _attention(q_arr, k_arr, v_arr, qcol, kcol, vcol, kv_width, bias, length, stride, halo,
               pair_heads, kv_div, sink=None, prev=None, want_lse=False, out_dtype=F32):
    b = q_arr.shape[0]
    q = ATT_Q
    nb = length // q
    assert nb >= 2 and length % q == 0 and q % halo == 0
    qh = q // halo
    n_hblocks = length // halo
    nk = q + 2 * halo
    width = N_HEADS * HEAD_DIM

    def main_map(col):
        return lambda bi, c, i: (bi, i, col(c))

    def prev_map(col):
        return lambda bi, c, i: (bi, jnp.maximum(i * qh - 1, 0), col(c))

    def next_map(col):
        return lambda bi, c, i: (bi, jnp.minimum((i + 1) * qh, n_hblocks - 1), col(c))

    in_specs = [pl.BlockSpec((1, q, width), main_map(qcol))]
    args = [q_arr]
    for arr, col in ((k_arr, kcol), (v_arr, vcol)):
        in_specs += [pl.BlockSpec((1, halo, kv_width), prev_map(col)),
                     pl.BlockSpec((1, q, kv_width), main_map(col)),
                     pl.BlockSpec((1, halo, kv_width), next_map(col))]
        args += [arr, arr, arr]
    in_specs.append(_resident(bias.shape))
    args.append(bias)
    if sink is not None:
        in_specs.append(pl.BlockSpec((1, LANES), lambda bi, c, i: (0, 0)))
        args.append(sink)
    if prev is not None:
        in_specs += [pl.BlockSpec((1, q, width), main_map(lambda c: c)),
                     pl.BlockSpec((1, q, LANES), main_map(lambda c: c))]
        args += list(prev)
    out_specs = [pl.BlockSpec((1, q, width), main_map(lambda c: c))]
    out_shape = [jax.ShapeDtypeStruct((b, length, stride * width), out_dtype)]
    if want_lse:
        out_specs.append(pl.BlockSpec((1, q, LANES), main_map(lambda c: c)))
        out_shape.append(jax.ShapeDtypeStruct((b, length, stride * LANES), F32))
    return pl.pallas_call(
        functools.partial(_attn_kernel, q_rows=q, halo=halo, pair_heads=tuple(pair_heads),
                          kv_div=kv_div, has_sink=sink is not None, has_prev=prev is not None,
                          want_lse=want_lse, nb=nb),
        grid=(b, stride, nb),
        in_specs=in_specs,
        out_specs=out_specs,
        out_shape=out_shape,
        scratch_shapes=[pltpu.VMEM((nk, kv_width), BF16), pltpu.VMEM((nk, kv_width), BF16),
                        pltpu.VMEM((len(pair_heads), 2 * q, LANES), F32)],
        compiler_params=_cparams(3),
        name="banded_attention",
    )(*args)


def _t5_bucket(rel):
    half = NUM_BUCKETS // 2
    exact = half // 2
    n = np.abs(rel)
    log_ratio = np.log(np.maximum(n, 1) / exact) / math.log(MAX_DISTANCE / exact)
    large = np.minimum(exact + (log_ratio * (half - exact)).astype(np.int64), half - 1)
    return np.where(rel > 0, half, 0) + np.where(n < exact, n, large)


def _attn_bias(rel_bias, halo, stride, pair_heads):
    q = ATT_Q
    nk = q + 2 * halo
    i = np.arange(q)[:, None]
    j = np.arange(nk)[None, :]
    rel = (j - halo) - i
    band = np.abs(rel) <= halo
    heads = np.asarray(pair_heads).reshape(-1)
    onehot = (_t5_bucket(rel * stride)[None] == np.arange(NUM_BUCKETS)[:, None, None])
    table = jnp.einsum("hb,bqk->hqk", rel_bias.astype(F32).T[heads], jnp.asarray(onehot).astype(F32),
                       precision=lax.Precision.HIGHEST).reshape(len(pair_heads), 2 * q, nk)
    valid = np.stack([band & (j >= halo), band, band & (j < halo + q)])
    valid = np.concatenate([valid, valid], axis=1)[:, None]
    return jnp.where(valid, table[None], NEG_INF)


SSD_HALO = BF16_ROWS
SSD_EXT = CHUNK + 2 * SSD_HALO


def _ssd_kernel(*refs, direction, nc):
    d = direction
    if d == 0:
        (xp_ref, x_ref, xn_ref, cw_ref, cb_ref, dsk_ref, dt_ref, dtb_ref, a_ref,
         y_ref, xo_ref, xe_ref, xc_ref, st_ref) = refs
    else:
        xi_ref, yin_ref, dt_ref, dtb_ref, a_ref, y_ref, xc_ref, st_ref = refs
    c = pl.program_id(1)

    @pl.when(c == 0)
    def _():
        st_ref[...] = jnp.zeros_like(st_ref)

    if d == 0:
        fprev = (c != 0).astype(F32)
        fnext = (c != nc - 1).astype(F32)
        h0 = SSD_HALO
        xe_ref[0:h0] = xp_ref[0].astype(F32) * fprev
        xe_ref[h0:h0 + CHUNK] = x_ref[0].astype(F32)
        xe_ref[h0 + CHUNK:] = xn_ref[0].astype(F32) * fnext
        cchunk = 512
        for j in range(0, CONV_DIM, cchunk):
            xe = xe_ref[:, j:j + cchunk]
            w = cw_ref[:, j:j + cchunk]
            t = (w[0:1] * pltpu.roll(xe, 2, 0)[h0:h0 + CHUNK]
                 + w[1:2] * pltpu.roll(xe, 1, 0)[h0:h0 + CHUNK]
                 + w[2:3] * xe[h0:h0 + CHUNK]
                 + w[3:4] * pltpu.roll(xe, SSD_EXT - 1, 0)[h0:h0 + CHUNK]) + cb_ref[:, j:j + cchunk]
            xc = _silu(t)
            xc_ref[:, j:j + cchunk] = xc
            xo_ref[0, :, j:j + cchunk] = xc.astype(xo_ref.dtype)
    else:
        xc_ref[...] = xi_ref[0].astype(F32)

    dtr = dt_ref[0] + dtb_ref[...]
    dt = jnp.maximum(dtr, 0.0) + jnp.log1p(jnp.exp(-jnp.abs(dtr)))
    adt = dt * a_ref[...]
    dt_t = dt.T
    adt_t = adt.T
    li = lax.broadcasted_iota(jnp.int32, (CHUNK, CHUNK), 0)
    ui = lax.broadcasted_iota(jnp.int32, (CHUNK, CHUNK), 1)
    mask = (ui <= li) if d == 0 else (ui >= li)
    tri = mask.astype(F32)
    hp = lax.Precision.HIGHEST
    acs = jnp.dot(tri, adt, precision=hp, preferred_element_type=F32)
    acs_t = lax.dot_general(adt_t, tri, (((1,), (1,)), ((), ())), precision=hp,
                            preferred_element_type=F32)
    tot = jnp.sum(adt_t, axis=1, keepdims=True)
    acs2 = acs * LOG2E
    tot2 = tot * LOG2E
    src2_t = acs_t * LOG2E - jnp.log2(dt_t)
    lo = lax.broadcasted_iota(jnp.int32, (CHUNK, LANES), 1) < HEAD_DIM

    heads_per_group = SSM_HEADS // SSM_GROUPS
    pairs_per_group = heads_per_group // 2
    for grp in range(SSM_GROUPS):
        bcol = D_INNER + grp * D_STATE
        ccol = D_INNER + SSM_GROUPS * D_STATE + grp * D_STATE
        bg = xc_ref[:, bcol:bcol + D_STATE]
        cg = xc_ref[:, ccol:ccol + D_STATE].astype(BF16)
        cb = lax.dot_general(cg, bg.astype(BF16), (((1,), (1,)), ((), ())),
                             preferred_element_type=F32)
        bg_t = bg.T
        for pp in range(pairs_per_group):
            p = grp * pairs_per_group + pp
            cols = slice(p * LANES, (p + 1) * LANES)
            x2 = xc_ref[:, cols]
            x2b = x2.astype(BF16)
            zero = jnp.zeros_like(x2b)
            xbd = jnp.concatenate([jnp.where(lo, x2b, zero), jnp.where(lo, zero, x2b)], axis=0)
            ms, bts, es, cds = [], [], [], []
            for hh in (d * SSM_HEADS + 2 * p, d * SSM_HEADS + 2 * p + 1):
                a_col = jnp.broadcast_to(acs2[:, hh:hh + 1], (CHUNK, CHUNK))
                src_row = src2_t[hh:hh + 1, :]
                lmat_dt = jnp.exp2(jnp.where(mask, a_col - src_row, NEG_INF))
                ms.append((cb * lmat_dt).astype(BF16))
                w_row = jnp.exp2(tot2[hh:hh + 1, :] - src_row)
                bts.append((bg_t * w_row).astype(BF16))
                es.append(jnp.exp2(a_col))
                cds.append(jnp.exp2(tot2[hh:hh + 1, :]))
            st = st_ref[p]
            y = jnp.dot(jnp.concatenate(ms, axis=1), xbd, preferred_element_type=F32)
            y = y + jnp.dot(cg, st.astype(BF16), preferred_element_type=F32) * jnp.where(lo, es[0], es[1])
            if d == 0:
                y = y + x2 * dsk_ref[:, cols]
            else:
                y = y + yin_ref[0, :, cols]
            y_ref[0, :, cols] = y
            st_ref[p] = (st * jnp.where(lo, cds[0], cds[1])
                         + jnp.dot(jnp.concatenate(bts, axis=1), xbd, preferred_element_type=F32))


def _ssd(x_in, dt_raw, dt_bias, a_neg, direction, conv=None, y_in=None):
    b, s, _ = x_in.shape
    nc = s // CHUNK
    hb = CHUNK // SSD_HALO
    n_hblocks = s // SSD_HALO

    def chunk_map(bi, c):
        return (bi, c if direction == 0 else nc - 1 - c, 0)

    def const_map(bi, c):
        return (0, 0)

    small = [pl.BlockSpec((1, CHUNK, LANES), chunk_map),
             pl.BlockSpec((1, LANES), const_map), pl.BlockSpec((1, LANES), const_map)]
    y_spec = pl.BlockSpec((1, CHUNK, D_INNER), chunk_map)
    x_spec = pl.BlockSpec((1, CHUNK, CONV_DIM), chunk_map)
    y_shape = jax.ShapeDtypeStruct((b, s, D_INNER), F32)
    scratch = [pltpu.VMEM((CHUNK, CONV_DIM), F32), pltpu.VMEM((SSM_HEADS // 2, D_STATE, LANES), F32)]
    if direction == 0:
        conv_w, conv_b, d_skip = conv
        in_specs = [
            pl.BlockSpec((1, SSD_HALO, CONV_DIM), lambda bi, c: (bi, jnp.maximum(c * hb - 1, 0), 0)),
            x_spec,
            pl.BlockSpec((1, SSD_HALO, CONV_DIM),
                         lambda bi, c: (bi, jnp.minimum((c + 1) * hb, n_hblocks - 1), 0)),
            pl.BlockSpec(conv_w.shape, const_map),
            pl.BlockSpec((1, CONV_DIM), const_map),
            pl.BlockSpec((1, D_INNER), const_map)] + small
        args = [x_in, x_in, x_in, conv_w, conv_b.reshape(1, CONV_DIM), d_skip, dt_raw, dt_bias, a_neg]
        out_specs = [y_spec, x_spec]
        out_shape = [y_shape, jax.ShapeDtypeStruct((b, s, CONV_DIM), BF16)]
        scratch = [pltpu.VMEM((SSD_EXT, CONV_DIM), F32)] + scratch
    else:
        in_specs = [x_spec, y_spec] + small
        args = [x_in, y_in, dt_raw, dt_bias, a_neg]
        out_specs = y_spec
        out_shape = y_shape
    return pl.pallas_call(
        functools.partial(_ssd_kernel, direction=direction, nc=nc),
        grid=(b, nc),
        in_specs=in_specs,
        out_specs=out_specs,
        out_shape=out_shape,
        scratch_shapes=scratch,
        compiler_params=_cparams(2),
        name="ssd_scan",
    )(*args)


A_PAIR_HEADS = tuple((2 * p, 2 * p + 1) for p in range(N_HEADS // 2))
B_PAIR_HEADS = tuple((8 * t + j, 8 * t + 4 + j) for t in range(2) for j in range(4))


def _prepare(p):
    aw = N_HEADS * HEAD_DIM
    scale = HEAD_DIM ** -0.5
    out = {}
    wa = p["a_w_qkv"][0].reshape(D_MODEL, N_DIL, 3, aw)
    wa = wa * jnp.asarray([scale, 1.0, 1.0], F32)[None, None, :, None]
    out["a_w_qkv"] = [wa[:, gi].reshape(D_MODEL, 3 * aw).astype(BF16) for gi in range(N_DIL)]
    out["a_w_o"] = p["a_w_o"][0].astype(BF16)
    perm = np.asarray(B_PAIR_HEADS).reshape(-1)
    cols = (perm[:, None] * HEAD_DIM + np.arange(HEAD_DIM)[None, :]).reshape(-1)
    wb = p["b_w_qkv"][0]
    out["b_w_qkv"] = jnp.concatenate([wb[:, :aw][:, cols] * scale, wb[:, aw:]], axis=1).astype(BF16)
    out["b_w_o"] = p["b_w_o"][0][cols, :].astype(BF16)
    out["b_sink"] = jnp.concatenate([p["b_sink"][0].astype(F32),
                                     jnp.zeros((LANES - N_HEADS,), F32)]).reshape(1, LANES)
    wc = p["c_w_in"][0].astype(BF16)
    out["c_w"] = (_chunk_cols(wc[:, :D_MODEL]), _chunk_cols(wc[:, D_MODEL:2 * D_MODEL]),
                  _chunk_cols(wc[:, 2 * D_MODEL:]), _chunk_cols(p["c_conv_w"][0].astype(F32)),
                  p["c_w_out"][0].astype(BF16).reshape(D_MODEL // MID_CHUNK, MID_CHUNK, D_MODEL))
    wd = p["d_w_in"][0]
    pad = jnp.zeros((D_MODEL, LANES - 2 * SSM_HEADS), F32)
    out["d_w_in"] = jnp.concatenate([wd, pad], axis=1).astype(BF16)
    lane_pad = jnp.zeros((LANES - 2 * SSM_HEADS,), F32)
    out["d_conv_w"] = p["d_conv_w"][0].astype(F32)
    out["d_conv_b"] = p["d_conv_b"][0].astype(F32)
    out["d_dt_bias"] = jnp.concatenate([p["d_dt_bias"][0].reshape(-1).astype(F32), lane_pad]).reshape(1, LANES)
    out["d_a_neg"] = jnp.concatenate([-jnp.exp(p["d_a_log"][0].reshape(-1).astype(F32)), lane_pad]).reshape(1, LANES)
    out["d_skip"] = jnp.repeat(p["d_skip"][0].astype(F32), HEAD_DIM).reshape(1, D_INNER)
    out["d_norm_g"] = p["d_norm_g"][0].astype(F32)
    out["d_w_out"] = p["d_w_out"][0].astype(BF16)
    ffn = []
    for i in range(DEPTH):
        wu = p["ffn_w_up"][i].astype(BF16)
        ffn.append((_chunk_cols(wu[:, :D_FF]), _chunk_cols(wu[:, D_FF:]),
                    _chunk_cols(p["ffn_conv_w"][i].astype(F32)),
                    _chunk_cols(p["ffn_conv_b"][i].astype(F32).reshape(1, D_FF)),
                    p["ffn_w_down"][i].astype(BF16).reshape(D_FF // MID_CHUNK, MID_CHUNK, D_MODEL)))
    out["ffn"] = ffn
    out["a_bias"] = [_attn_bias(p["rel_bias"], window // (2 * r), r, A_PAIR_HEADS)
                     for window, r in DILATIONS]
    out["b_bias"] = _attn_bias(p["rel_bias"], B_HALF_WINDOW, 1, B_PAIR_HEADS)
    out["norm_g"] = p["norm_g"].astype(F32)
    out["final_g"] = p["final_g"].astype(F32)
    return out


def _mixer_dilated(x2, g, w, bsz, seq):
    aw = N_HEADS * HEAD_DIM
    d = x2.shape[1]
    prev = None
    prev_r = 1
    for gi, (window, r) in enumerate(DILATIONS):
        length = seq // r
        xr = x2 if r == 1 else x2.reshape(bsz, length, r, d).transpose(0, 2, 1, 3).reshape(bsz * seq, d)
        (qkv,) = _norm_proj(xr, g, w["a_w_qkv"][gi], [(0, 3 * aw)], [BF16])
        view = qkv.reshape(bsz * r, length, 3 * aw)
        if prev is not None:
            k = r // prev_r

            def regroup(t):
                t = t.reshape(bsz, prev_r, length, k, t.shape[-1]).transpose(0, 3, 1, 2, 4)
                return t.reshape(bsz * r, length, t.shape[-1])

            prev = (regroup(prev[0]), regroup(prev[1]))
        res = _attention(
            view, view, view, qcol=lambda c: 0, kcol=lambda c: 1, vcol=lambda c: 2,
            kv_width=aw, bias=w["a_bias"][gi], length=length, stride=1, halo=window // (2 * r),
            pair_heads=A_PAIR_HEADS, kv_div=1, prev=prev, want_lse=True, out_dtype=BF16)
        prev = tuple(res)
        prev_r = r
    o = prev[0].reshape(bsz, prev_r, seq // prev_r, aw).transpose(0, 2, 1, 3)
    return _out_proj(x2, o.reshape(bsz * seq, aw), w["a_w_o"])


def _mixer_window(x2, g, w, bsz, seq):
    aw = N_HEADS * HEAD_DIM
    kvw = B_KV_HEADS * HEAD_DIM
    n = aw + 2 * kvw
    (qkv,) = _norm_proj(x2, g, w["b_w_qkv"], [(0, n)], [BF16])
    view = qkv.reshape(bsz, seq, n)
    (o,) = _attention(view[:, :, :aw], view, view,
                      qcol=lambda c: 0, kcol=lambda c: aw // kvw, vcol=lambda c: aw // kvw + 1,
                      kv_width=kvw, bias=w["b_bias"], length=seq, stride=1, halo=B_HALF_WINDOW,
                      pair_heads=B_PAIR_HEADS, kv_div=4, sink=w["b_sink"], out_dtype=BF16)
    return _out_proj(x2, o.reshape(bsz * seq, aw), w["b_w_o"])


def _mixer_ssd(x2, g, w, bsz, seq):
    z, xbc, dt_raw = _norm_proj(
        x2, g, w["d_w_in"],
        [(0, D_INNER), (D_INNER, CONV_DIM), (D_INNER + CONV_DIM, LANES)], [BF16, BF16, F32])
    xbc = xbc.reshape(bsz, seq, CONV_DIM)
    dt_raw = dt_raw.reshape(bsz, seq, LANES)
    y, xc = _ssd(xbc, dt_raw, w["d_dt_bias"], w["d_a_neg"], 0,
                 conv=(w["d_conv_w"], w["d_conv_b"], w["d_skip"]))
    y = _ssd(xc, dt_raw, w["d_dt_bias"], w["d_a_neg"], 1, y_in=y)
    return _ssd_out(x2, y.reshape(bsz * seq, D_INNER), z, w["d_norm_g"], w["d_w_out"])


def _trunk(x, w):
    bsz, seq, d = x.shape
    x2 = x.reshape(bsz * seq, d)
    ng = w["norm_g"]
    for i in range(DEPTH):
        kind = i % 4
        if kind == 0:
            x2 = _mixer_dilated(x2, ng[i, 0], w, bsz, seq)
        elif kind == 1:
            x2 = _mixer_window(x2, ng[i, 0], w, bsz, seq)
        elif kind == 2:
            x2 = _gated_mlp(x2, ng[i, 0], w["c_w"], "sconv", seq)
        else:
            x2 = _mixer_ssd(x2, ng[i, 0], w, bsz, seq)
        x2 = _gated_mlp(x2, ng[i, 1], w["ffn"][i], "ffn", seq,
                        final_g=w["final_g"] if i == DEPTH - 1 else None)
    return x2.reshape(bsz, seq, d)


def kernel(x_prompt, x_sample, rel_bias, a_w_qkv, a_w_o, b_w_qkv, b_sink, b_w_o, c_w_in, c_conv_w,
           c_w_out, d_w_in, d_conv_w, d_conv_b, d_dt_bias, d_a_log, d_skip, d_norm_g, d_w_out,
           ffn_w_up, ffn_conv_w, ffn_conv_b, ffn_w_down, norm_g, final_g):
    w = _prepare(dict(
        rel_bias=rel_bias, a_w_qkv=a_w_qkv, a_w_o=a_w_o, b_w_qkv=b_w_qkv, b_sink=b_sink, b_w_o=b_w_o,
        c_w_in=c_w_in, c_conv_w=c_conv_w, c_w_out=c_w_out, d_w_in=d_w_in, d_conv_w=d_conv_w,
        d_conv_b=d_conv_b, d_dt_bias=d_dt_bias, d_a_log=d_a_log, d_skip=d_skip, d_norm_g=d_norm_g,
        d_w_out=d_w_out, ffn_w_up=ffn_w_up, ffn_conv_w=ffn_conv_w, ffn_conv_b=ffn_conv_b,
        ffn_w_down=ffn_w_down, norm_g=norm_g, final_g=final_g))
    return (_trunk(x_prompt, w), _trunk(x_sample, w))
```

```python
import functools
import math

import numpy as np
import jax
import jax.numpy as jnp
from jax import lax
from jax.experimental import pallas as pl
from jax.experimental.pallas import tpu as pltpu

F32 = jnp.float32
BF16 = jnp.bfloat16

D_MODEL = 1024
DEPTH = 4
HEAD_DIM = 64
N_HEADS = 16
DILATIONS = ((128, 1), (512, 4), (2048, 16))
N_DIL = 3
B_KV_HEADS = 4
B_HALF_WINDOW = 128
NUM_BUCKETS = 32
MAX_DISTANCE = 1024
D_INNER = 2 * D_MODEL
SSM_HEADS = 32
SSM_GROUPS = 4
D_STATE = 128
CHUNK = 128
CONV_DIM = D_INNER + 2 * SSM_GROUPS * D_STATE
D_FF = 2816
EPS = 1e-6
NEG_INF = -1e30
LOG2E = 1.0 / math.log(2.0)

LANES = 128
BF16_ROWS = 16
VMEM_LIMIT = 56 * 1024 * 1024
ROW_TILE = 512
MID_CHUNK = 256
ATT_Q = 128


def _cparams(n_axes):
    return pltpu.CompilerParams(dimension_semantics=("arbitrary",) * n_axes,
                                vmem_limit_bytes=VMEM_LIMIT)


def _resident(shape):
    nd = len(shape)
    return pl.BlockSpec(shape, lambda *_: (0,) * nd, pipeline_mode=pl.Buffered(1))


def _rms(x, g):
    return x * lax.rsqrt(jnp.mean(x * x, axis=-1, keepdims=True) + EPS) * g


def _silu(t):
    return t / (1.0 + jnp.exp(-t))


def _norm_proj_kernel(x_ref, g_ref, w_ref, *rest, segs, col_chunk):
    o_refs, h_ref = rest[:-1], rest[-1]
    h_ref[...] = _rms(x_ref[...], g_ref[...]).astype(BF16)
    for o_ref, (start, width) in zip(o_refs, segs):
        for c in range(0, width, col_chunk):
            cw = min(col_chunk, width - c)
            o_ref[:, c:c + cw] = jnp.dot(h_ref[...], w_ref[:, start + c:start + c + cw],
                                         preferred_element_type=F32).astype(o_ref.dtype)


def _norm_proj(x2, g, w, segs, dtypes, tm=ROW_TILE):
    m, d = x2.shape
    n = w.shape[1]
    out_shape = [jax.ShapeDtypeStruct((m, wd), dt) for (_, wd), dt in zip(segs, dtypes)]
    return pl.pallas_call(
        functools.partial(_norm_proj_kernel, segs=tuple(segs), col_chunk=512),
        grid=(m // tm,),
        in_specs=[pl.BlockSpec((tm, d), lambda i: (i, 0)),
                  pl.BlockSpec((1, d), lambda i: (0, 0)),
                  _resident((d, n))],
        out_specs=[pl.BlockSpec((tm, wd), lambda i: (i, 0)) for (_, wd) in segs],
        out_shape=out_shape,
        scratch_shapes=[pltpu.VMEM((tm, d), BF16)],
        compiler_params=_cparams(1),
        name="norm_proj",
    )(x2, g.reshape(1, d), w)


def _norm_proj_cm_kernel(x_ref, g_ref, w_ref, o_ref, hf_ref, h_ref, *, r, col_chunk):
    tm, d = x_ref.shape
    rows = tm // r
    hn = _rms(x_ref[...], g_ref[...])
    for s in range(d // LANES):
        hf_ref[s] = hn[:, s * LANES:(s + 1) * LANES]
    for s in range(d // LANES):
        for c in range(r):
            h_ref[c * rows:(c + 1) * rows, s * LANES:(s + 1) * LANES] = (
                hf_ref[s, pl.ds(c, rows, stride=r), :].astype(BF16))
    n = w_ref.shape[1]
    for j in range(0, n, col_chunk):
        res = jnp.dot(h_ref[...], w_ref[:, j:j + col_chunk],
                      preferred_element_type=F32).astype(o_ref.dtype)
        for c in range(r):
            o_ref[0, c, :, j:j + col_chunk] = res[c * rows:(c + 1) * rows]


def _norm_proj_class_major(x2, g, w, bsz, seq, r, tm=ROW_TILE):
    m, d = x2.shape
    n = w.shape[1]
    tiles = seq // tm
    rows = tm // r
    assert rows % BF16_ROWS == 0
    return pl.pallas_call(
        functools.partial(_norm_proj_cm_kernel, r=r, col_chunk=512),
        grid=(m // tm,),
        in_specs=[pl.BlockSpec((tm, d), lambda i: (i, 0)),
                  pl.BlockSpec((1, d), lambda i: (0, 0)),
                  _resident((d, n))],
        out_specs=pl.BlockSpec((1, r, rows, n), lambda i: (i // tiles, 0, i % tiles, 0)),
        out_shape=jax.ShapeDtypeStruct((bsz, r, seq // r, n), BF16),
        scratch_shapes=[pltpu.VMEM((d // LANES, tm, LANES), F32), pltpu.VMEM((tm, d), BF16)],
        compiler_params=_cparams(1),
        name="norm_proj_class_major",
    )(x2, g.reshape(1, d), w)


HALO = BF16_ROWS


def _conv3(t, w, tm):
    n = tm + 2 * HALO
    up = pltpu.roll(t, 1, 0)[HALO:HALO + tm]
    dn = pltpu.roll(t, n - 1, 0)[HALO:HALO + tm]
    return w[0:1] * up + w[1:2] * t[HALO:HALO + tm] + w[2:3] * dn


def _gated_mlp_kernel(xp_ref, x_ref, xn_ref, g_ref, *rest, mode, tm, tiles_per_seq, nchunks,
                      final_norm):
    if mode == "ffn":
        wa_ref, wu_ref, cw_ref, cb_ref, wd_ref, fg_ref, o_ref, h_ref, acc_ref = rest
    else:
        wb_ref, wc_ref, wx_ref, cw_ref, wd_ref, fg_ref, o_ref, h_ref, acc_ref = rest
    pos = pl.program_id(0) % tiles_per_seq
    fprev = (pos != 0).astype(F32)
    fnext = (pos != tiles_per_seq - 1).astype(F32)
    g = g_ref[...]
    h_ref[0:HALO, :] = (_rms(xp_ref[...], g) * fprev).astype(BF16)
    h_ref[HALO:HALO + tm, :] = _rms(x_ref[...], g).astype(BF16)
    h_ref[HALO + tm:, :] = (_rms(xn_ref[...], g) * fnext).astype(BF16)
    acc_ref[...] = jnp.zeros_like(acc_ref)

    def body(c, carry):
        h_ext = h_ref[...]
        h_main = h_ref[HALO:HALO + tm, :]
        if mode == "ffn":
            a = jnp.dot(h_ext, wa_ref[c], preferred_element_type=F32)
            u = jnp.dot(h_main, wu_ref[c], preferred_element_type=F32)
            t = _conv3(a, cw_ref[c], tm) + cb_ref[c]
            gt = _silu(t) * u
        else:
            cg = jnp.dot(h_ext, wc_ref[c], preferred_element_type=F32)
            xi = jnp.dot(h_ext, wx_ref[c], preferred_element_type=F32)
            bg = jnp.dot(h_main, wb_ref[c], preferred_element_type=F32)
            gt = bg * _conv3(cg * xi, cw_ref[c], tm)
        acc_ref[...] += jnp.dot(gt.astype(BF16), wd_ref[c], preferred_element_type=F32)
        return carry

    lax.fori_loop(0, nchunks, body, 0, unroll=True)
    y = x_ref[...] + acc_ref[...]
    if final_norm:
        y = _rms(y, fg_ref[...])
    o_ref[...] = y


def _chunk_cols(w, cw=MID_CHUNK):
    k, n = w.shape
    return w.reshape(k, n // cw, cw).transpose(1, 0, 2)


def _gated_mlp(x2, g, weights, mode, seq, final_g=None, tm=ROW_TILE):
    m, d = x2.shape
    nchunks = weights[-1].shape[0]
    hb = tm // HALO
    n_hblocks = m // HALO
    in_specs = [
        pl.BlockSpec((HALO, d), lambda i: (jnp.maximum(i * hb - 1, 0), 0)),
        pl.BlockSpec((tm, d), lambda i: (i, 0)),
        pl.BlockSpec((HALO, d), lambda i: (jnp.minimum((i + 1) * hb, n_hblocks - 1), 0)),
        pl.BlockSpec((1, d), lambda i: (0, 0)),
    ] + [_resident(w.shape) for w in weights] + [pl.BlockSpec((1, d), lambda i: (0, 0))]
    fg = (final_g if final_g is not None else jnp.ones((d,), F32)).reshape(1, d)
    return pl.pallas_call(
        functools.partial(_gated_mlp_kernel, mode=mode, tm=tm, tiles_per_seq=seq // tm,
                          nchunks=nchunks, final_norm=final_g is not None),
        grid=(m // tm,),
        in_specs=in_specs,
        out_specs=pl.BlockSpec((tm, d), lambda i: (i, 0)),
        out_shape=jax.ShapeDtypeStruct((m, d), F32),
        scratch_shapes=[pltpu.VMEM((tm + 2 * HALO, d), BF16), pltpu.VMEM((tm, d), F32)],
        compiler_params=_cparams(1),
        name="gated_mlp_" + mode,
    )(x2, x2, x2, g.reshape(1, d), *weights, fg)


def _out_proj_kernel(x_ref, o_ref, w_ref, y_ref):
    y_ref[...] = x_ref[...] + jnp.dot(o_ref[...].astype(BF16), w_ref[...],
                                      preferred_element_type=F32)


def _out_proj(x2, o2, w, tm=ROW_TILE):
    m, d = x2.shape
    k = o2.shape[1]
    return pl.pallas_call(
        _out_proj_kernel,
        grid=(m // tm,),
        in_specs=[pl.BlockSpec((tm, d), lambda i: (i, 0)),
                  pl.BlockSpec((tm, k), lambda i: (i, 0)),
                  _resident((k, d))],
        out_specs=pl.BlockSpec((tm, d), lambda i: (i, 0)),
        out_shape=jax.ShapeDtypeStruct((m, d), F32),
        compiler_params=_cparams(1),
        name="out_proj",
    )(x2, o2, w)


def _ssd_out_kernel(x_ref, y_ref, z_ref, g_ref, w_ref, o_ref):
    yv = y_ref[...] * _silu(z_ref[...].astype(F32))
    hn = _rms(yv, g_ref[...]).astype(BF16)
    o_ref[...] = x_ref[...] + jnp.dot(hn, w_ref[...], preferred_element_type=F32)


def _ssd_out(x2, y2, z2, g, w, tm=ROW_TILE):
    m, d = x2.shape
    k = y2.shape[1]
    return pl.pallas_call(
        _ssd_out_kernel,
        grid=(m // tm,),
        in_specs=[pl.BlockSpec((tm, d), lambda i: (i, 0)),
                  pl.BlockSpec((tm, k), lambda i: (i, 0)),
                  pl.BlockSpec((tm, k), lambda i: (i, 0)),
                  pl.BlockSpec((1, k), lambda i: (0, 0)),
                  _resident((k, d))],
        out_specs=pl.BlockSpec((tm, d), lambda i: (i, 0)),
        out_shape=jax.ShapeDtypeStruct((m, d), F32),
        compiler_params=_cparams(1),
        name="ssd_out",
    )(x2, y2, z2, g.reshape(1, k), w)


def _attn_kernel(*refs, q_rows, halo, pair_heads, kv_div, has_sink, has_prev, want_lse, nb):
    refs = list(refs)
    q_ref, kp_ref, k_ref, kn_ref, vp_ref, v_ref, vn_ref, bias_ref = refs[:8]
    refs = refs[8:]
    sink_ref = refs.pop(0) if has_sink else None
    if has_prev:
        op_ref, lp_ref = refs.pop(0), refs.pop(0)
    o_ref = refs.pop(0)
    lse_ref = refs.pop(0) if want_lse else None
    kcat, vcat, pv_ref = refs
    q, h = q_rows, halo
    i = pl.program_id(2)
    variant = jnp.where(i == 0, 0, jnp.where(i == nb - 1, 2, 1))
    kcat[0:h] = kp_ref[0]
    kcat[h:h + q] = k_ref[0]
    kcat[h + q:] = kn_ref[0]
    vcat[0:h] = vp_ref[0]
    vcat[h:h + q] = v_ref[0]
    vcat[h + q:] = vn_ref[0]
    lane = lax.broadcasted_iota(jnp.int32, (q, LANES), 1)
    lo = lane < HEAD_DIM
    lo_b = (lax.broadcasted_iota(jnp.int32, (1, LANES), 1) < HEAD_DIM).astype(BF16)
    hi_b = 1.0 - lo_b
    m_tile = jnp.zeros((q, LANES), F32)
    l_tile = jnp.ones((q, LANES), F32)
    for p, heads in enumerate(pair_heads):
        cols = slice(p * LANES, (p + 1) * LANES)
        kcols = slice((p // kv_div) * LANES, (p // kv_div + 1) * LANES)
        q2 = q_ref[0, :, cols]
        qs = jnp.concatenate([q2 * lo_b, q2 * hi_b], axis=0)
        s = lax.dot_general(qs, kcat[:, kcols], (((1,), (1,)), ((), ())),
                            preferred_element_type=F32) + bias_ref[variant, p]
        m = jnp.max(s, axis=-1, keepdims=True)
        e = jnp.exp(s - m)
        l = jnp.sum(e, axis=-1, keepdims=True)
        pv_ref[p] = jnp.dot(e.astype(BF16), vcat[:, kcols], preferred_element_type=F32)
        h_lo, h_hi = heads
        m_tile = jnp.where(lane == h_lo, m[:q], jnp.where(lane == h_hi, m[q:], m_tile))
        l_tile = jnp.where(lane == h_lo, l[:q], jnp.where(lane == h_hi, l[q:], l_tile))
    if has_sink:
        sink_row = sink_ref[...]
        m_all = jnp.maximum(m_tile, sink_row)
        rescale = jnp.exp(m_tile - m_all)
        l_tile = l_tile * rescale + jnp.exp(sink_row - m_all)
        w_cur = rescale / l_tile
        m_tile = m_all
    else:
        w_cur = 1.0 / l_tile
    if want_lse or has_prev:
        lse = m_tile + jnp.log(l_tile)
    if has_prev:
        lp = lp_ref[0]
        n = jnp.maximum(lp, lse)
        n = n + jnp.log(jnp.exp(lp - n) + jnp.exp(lse - n))
        w_cur = w_cur * jnp.exp(lse - n)
        w_prev = jnp.exp(lp - n)
        lse = n
    if want_lse:
        lse_ref[0] = lse
    for p, (h_lo, h_hi) in enumerate(pair_heads):
        cols = slice(p * LANES, (p + 1) * LANES)
        pv = pv_ref[p]
        o = jnp.where(lo, pv[:q] * w_cur[:, h_lo:h_lo + 1], pv[q:] * w_cur[:, h_hi:h_hi + 1])
        if has_prev:
            o = o + op_ref[0, :, cols] * jnp.where(lo, w_prev[:, h_lo:h_lo + 1],
                                                   w_prev[:, h_hi:h_hi + 1])
        o_ref[0, :, cols] = o.astype(o_ref.dtype)


def _attention(q_arr, k_arr, v_arr, qcol, kcol, vcol, kv_width, bias, length, stride, halo,
               pair_heads, kv_div, sink=None, prev=None, want_lse=False, out_dtype=F32):
    b = q_arr.shape[0]
    q = ATT_Q
    nb = length // q
    assert nb >= 2 and length % q == 0 and q % halo == 0
    qh = q // halo
    n_hblocks = length // halo
    nk = q + 2 * halo
    width = N_HEADS * HEAD_DIM

    def main_map(col):
        return lambda bi, c, i: (bi, i, col(c))

    def prev_map(col):
        return lambda bi, c, i: (bi, jnp.maximum(i * qh - 1, 0), col(c))

    def next_map(col):
        return lambda bi, c, i: (bi, jnp.minimum((i + 1) * qh, n_hblocks - 1), col(c))

    in_specs = [pl.BlockSpec((1, q, width), main_map(qcol))]
    args = [q_arr]
    for arr, col in ((k_arr, kcol), (v_arr, vcol)):
        in_specs += [pl.BlockSpec((1, halo, kv_width), prev_map(col)),
                     pl.BlockSpec((1, q, kv_width), main_map(col)),
                     pl.BlockSpec((1, halo, kv_width), next_map(col))]
        args += [arr, arr, arr]
    in_specs.append(_resident(bias.shape))
    args.append(bias)
    if sink is not None:
        in_specs.append(pl.BlockSpec((1, LANES), lambda bi, c, i: (0, 0)))
        args.append(sink)
    if prev is not None:
        in_specs += [pl.BlockSpec((1, q, width), main_map(lambda c: c)),
                     pl.BlockSpec((1, q, LANES), main_map(lambda c: c))]
        args += list(prev)
    out_specs = [pl.BlockSpec((1, q, width), main_map(lambda c: c))]
    out_shape = [jax.ShapeDtypeStruct((b, length, stride * width), out_dtype)]
    if want_lse:
        out_specs.append(pl.BlockSpec((1, q, LANES), main_map(lambda c: c)))
        out_shape.append(jax.ShapeDtypeStruct((b, length, stride * LANES), F32))
    return pl.pallas_call(
        functools.partial(_attn_kernel, q_rows=q, halo=halo, pair_heads=tuple(pair_heads),
                          kv_div=kv_div, has_sink=sink is not None, has_prev=prev is not None,
                          want_lse=want_lse, nb=nb),
        grid=(b, stride, nb),
        in_specs=in_specs,
        out_specs=out_specs,
        out_shape=out_shape,
        scratch_shapes=[pltpu.VMEM((nk, kv_width), BF16), pltpu.VMEM((nk, kv_width), BF16),
                        pltpu.VMEM((len(pair_heads), 2 * q, LANES), F32)],
        compiler_params=_cparams(3),
        name="banded_attention",
    )(*args)


def _t5_bucket(rel):
    half = NUM_BUCKETS // 2
    exact = half // 2
    n = np.abs(rel)
    log_ratio = np.log(np.maximum(n, 1) / exact) / math.log(MAX_DISTANCE / exact)
    large = np.minimum(exact + (log_ratio * (half - exact)).astype(np.int64), half - 1)
    return np.where(rel > 0, half, 0) + np.where(n < exact, n, large)


def _attn_bias(rel_bias, halo, stride, pair_heads):
    q = ATT_Q
    nk = q + 2 * halo
    i = np.arange(q)[:, None]
    j = np.arange(nk)[None, :]
    rel = (j - halo) - i
    band = np.abs(rel) <= halo
    heads = np.asarray(pair_heads).reshape(-1)
    onehot = (_t5_bucket(rel * stride)[None] == np.arange(NUM_BUCKETS)[:, None, None])
    table = jnp.einsum("hb,bqk->hqk", rel_bias.astype(F32).T[heads], jnp.asarray(onehot).astype(F32),
                       precision=lax.Precision.HIGHEST).reshape(len(pair_heads), 2 * q, nk)
    valid = np.stack([band & (j >= halo), band, band & (j < halo + q)])
    valid = np.concatenate([valid, valid], axis=1)[:, None]
    return jnp.where(valid, table[None], NEG_INF)


SSD_HALO = BF16_ROWS
SSD_EXT = CHUNK + 2 * SSD_HALO


def _ssd_kernel(*refs, direction, nc):
    d = direction
    if d == 0:
        (xp_ref, x_ref, xn_ref, cw_ref, cb_ref, dsk_ref, dt_ref, dtb_ref, a_ref,
         y_ref, xo_ref, xe_ref, xc_ref, st_ref) = refs
    else:
        xi_ref, yin_ref, dt_ref, dtb_ref, a_ref, y_ref, xc_ref, st_ref = refs
    c = pl.program_id(1)

    @pl.when(c == 0)
    def _():
        st_ref[...] = jnp.zeros_like(st_ref)

    if d == 0:
        fprev = (c != 0).astype(F32)
        fnext = (c != nc - 1).astype(F32)
        h0 = SSD_HALO
        xe_ref[0:h0] = xp_ref[0].astype(F32) * fprev
        xe_ref[h0:h0 + CHUNK] = x_ref[0].astype(F32)
        xe_ref[h0 + CHUNK:] = xn_ref[0].astype(F32) * fnext
        cchunk = 512
        for j in range(0, CONV_DIM, cchunk):
            xe = xe_ref[:, j:j + cchunk]
            w = cw_ref[:, j:j + cchunk]
            t = (w[0:1] * pltpu.roll(xe, 2, 0)[h0:h0 + CHUNK]
                 + w[1:2] * pltpu.roll(xe, 1, 0)[h0:h0 + CHUNK]
                 + w[2:3] * xe[h0:h0 + CHUNK]
                 + w[3:4] * pltpu.roll(xe, SSD_EXT - 1, 0)[h0:h0 + CHUNK]) + cb_ref[:, j:j + cchunk]
            xc = _silu(t)
            xc_ref[:, j:j + cchunk] = xc
            xo_ref[0, :, j:j + cchunk] = xc.astype(xo_ref.dtype)
    else:
        xc_ref[...] = xi_ref[0].astype(F32)

    dtr = dt_ref[0] + dtb_ref[...]
    dt = jnp.maximum(dtr, 0.0) + jnp.log1p(jnp.exp(-jnp.abs(dtr)))
    adt = dt * a_ref[...]
    dt_t = dt.T
    adt_t = adt.T
    li = lax.broadcasted_iota(jnp.int32, (CHUNK, CHUNK), 0)
    ui = lax.broadcasted_iota(jnp.int32, (CHUNK, CHUNK), 1)
    mask = (ui <= li) if d == 0 else (ui >= li)
    tri = mask.astype(F32)
    hp = lax.Precision.HIGHEST
    acs = jnp.dot(tri, adt, precision=hp, preferred_element_type=F32)
    acs_t = lax.dot_general(adt_t, tri, (((1,), (1,)), ((), ())), precision=hp,
                            preferred_element_type=F32)
    tot = jnp.sum(adt_t, axis=1, keepdims=True)
    acs2 = acs * LOG2E
    tot2 = tot * LOG2E
    src2_t = acs_t * LOG2E - jnp.log2(dt_t)
    lo = lax.broadcasted_iota(jnp.int32, (CHUNK, LANES), 1) < HEAD_DIM

    heads_per_group = SSM_HEADS // SSM_GROUPS
    pairs_per_group = heads_per_group // 2
    for grp in range(SSM_GROUPS):
        bcol = D_INNER + grp * D_STATE
        ccol = D_INNER + SSM_GROUPS * D_STATE + grp * D_STATE
        bg = xc_ref[:, bcol:bcol + D_STATE]
        cg = xc_ref[:, ccol:ccol + D_STATE].astype(BF16)
        cb = lax.dot_general(cg, bg.astype(BF16), (((1,), (1,)), ((), ())),
                             preferred_element_type=F32)
        bg_t = bg.T
        for pp in range(pairs_per_group):
            p = grp * pairs_per_group + pp
            cols = slice(p * LANES, (p + 1) * LANES)
            x2 = xc_ref[:, cols]
            x2b = x2.astype(BF16)
            zero = jnp.zeros_like(x2b)
            xbd = jnp.concatenate([jnp.where(lo, x2b, zero), jnp.where(lo, zero, x2b)], axis=0)
            ms, bts, es, cds = [], [], [], []
            for hh in (d * SSM_HEADS + 2 * p, d * SSM_HEADS + 2 * p + 1):
                a_col = jnp.broadcast_to(acs2[:, hh:hh + 1], (CHUNK, CHUNK))
                src_row = src2_t[hh:hh + 1, :]
                lmat_dt = jnp.exp2(jnp.where(mask, a_col - src_row, NEG_INF))
                ms.append((cb * lmat_dt).astype(BF16))
                w_row = jnp.exp2(tot2[hh:hh + 1, :] - src_row)
                bts.append((bg_t * w_row).astype(BF16))
                es.append(jnp.exp2(a_col))
                cds.append(jnp.exp2(tot2[hh:hh + 1, :]))
            st = st_ref[p]
            y = jnp.dot(jnp.concatenate(ms, axis=1), xbd, preferred_element_type=F32)
            y = y + jnp.dot(cg, st.astype(BF16), preferred_element_type=F32) * jnp.where(lo, es[0], es[1])
            if d == 0:
                y = y + x2 * dsk_ref[:, cols]
            else:
                y = y + yin_ref[0, :, cols]
            y_ref[0, :, cols] = y
            st_ref[p] = (st * jnp.where(lo, cds[0], cds[1])
                         + jnp.dot(jnp.concatenate(bts, axis=1), xbd, preferred_element_type=F32))


def _ssd(x_in, dt_raw, dt_bias, a_neg, direction, conv=None, y_in=None):
    b, s, _ = x_in.shape
    nc = s // CHUNK
    hb = CHUNK // SSD_HALO
    n_hblocks = s // SSD_HALO

    def chunk_map(bi, c):
        return (bi, c if direction == 0 else nc - 1 - c, 0)

    def const_map(bi, c):
        return (0, 0)

    small = [pl.BlockSpec((1, CHUNK, LANES), chunk_map),
             pl.BlockSpec((1, LANES), const_map), pl.BlockSpec((1, LANES), const_map)]
    y_spec = pl.BlockSpec((1, CHUNK, D_INNER), chunk_map)
    x_spec = pl.BlockSpec((1, CHUNK, CONV_DIM), chunk_map)
    y_shape = jax.ShapeDtypeStruct((b, s, D_INNER), F32)
    scratch = [pltpu.VMEM((CHUNK, CONV_DIM), F32), pltpu.VMEM((SSM_HEADS // 2, D_STATE, LANES), F32)]
    if direction == 0:
        conv_w, conv_b, d_skip = conv
        in_specs = [
            pl.BlockSpec((1, SSD_HALO, CONV_DIM), lambda bi, c: (bi, jnp.maximum(c * hb - 1, 0), 0)),
            x_spec,
            pl.BlockSpec((1, SSD_HALO, CONV_DIM),
                         lambda bi, c: (bi, jnp.minimum((c + 1) * hb, n_hblocks - 1), 0)),
            pl.BlockSpec(conv_w.shape, const_map),
            pl.BlockSpec((1, CONV_DIM), const_map),
            pl.BlockSpec((1, D_INNER), const_map)] + small
        args = [x_in, x_in, x_in, conv_w, conv_b.reshape(1, CONV_DIM), d_skip, dt_raw, dt_bias, a_neg]
        out_specs = [y_spec, x_spec]
        out_shape = [y_shape, jax.ShapeDtypeStruct((b, s, CONV_DIM), BF16)]
        scratch = [pltpu.VMEM((SSD_EXT, CONV_DIM), F32)] + scratch
    else:
        in_specs = [x_spec, y_spec] + small
        args = [x_in, y_in, dt_raw, dt_bias, a_neg]
        out_specs = y_spec
        out_shape = y_shape
    return pl.pallas_call(
        functools.partial(_ssd_kernel, direction=direction, nc=nc),
        grid=(b, nc),
        in_specs=in_specs,
        out_specs=out_specs,
        out_shape=out_shape,
        scratch_shapes=scratch,
        compiler_params=_cparams(2),
        name="ssd_scan",
    )(*args)


A_PAIR_HEADS = tuple((2 * p, 2 * p + 1) for p in range(N_HEADS // 2))
B_PAIR_HEADS = tuple((8 * t + j, 8 * t + 4 + j) for t in range(2) for j in range(4))


def _prepare(p):
    aw = N_HEADS * HEAD_DIM
    scale = HEAD_DIM ** -0.5
    out = {}
    wa = p["a_w_qkv"][0].reshape(D_MODEL, N_DIL, 3, aw)
    wa = wa * jnp.asarray([scale, 1.0, 1.0], F32)[None, None, :, None]
    out["a_w_qkv"] = [wa[:, gi].reshape(D_MODEL, 3 * aw).astype(BF16) for gi in range(N_DIL)]
    out["a_w_o"] = p["a_w_o"][0].astype(BF16)
    perm = np.asarray(B_PAIR_HEADS).reshape(-1)
    cols = (perm[:, None] * HEAD_DIM + np.arange(HEAD_DIM)[None, :]).reshape(-1)
    wb = p["b_w_qkv"][0]
    out["b_w_qkv"] = jnp.concatenate([wb[:, :aw][:, cols] * scale, wb[:, aw:]], axis=1).astype(BF16)
    out["b_w_o"] = p["b_w_o"][0][cols, :].astype(BF16)
    out["b_sink"] = jnp.concatenate([p["b_sink"][0].astype(F32),
                                     jnp.zeros((LANES - N_HEADS,), F32)]).reshape(1, LANES)
    wc = p["c_w_in"][0].astype(BF16)
    out["c_w"] = (_chunk_cols(wc[:, :D_MODEL]), _chunk_cols(wc[:, D_MODEL:2 * D_MODEL]),
                  _chunk_cols(wc[:, 2 * D_MODEL:]), _chunk_cols(p["c_conv_w"][0].astype(F32)),
                  p["c_w_out"][0].astype(BF16).reshape(D_MODEL // MID_CHUNK, MID_CHUNK, D_MODEL))
    wd = p["d_w_in"][0]
    pad = jnp.zeros((D_MODEL, LANES - 2 * SSM_HEADS), F32)
    out["d_w_in"] = jnp.concatenate([wd, pad], axis=1).astype(BF16)
    lane_pad = jnp.zeros((LANES - 2 * SSM_HEADS,), F32)
    out["d_conv_w"] = p["d_conv_w"][0].astype(F32)
    out["d_conv_b"] = p["d_conv_b"][0].astype(F32)
    out["d_dt_bias"] = jnp.concatenate([p["d_dt_bias"][0].reshape(-1).astype(F32), lane_pad]).reshape(1, LANES)
    out["d_a_neg"] = jnp.concatenate([-jnp.exp(p["d_a_log"][0].reshape(-1).astype(F32)), lane_pad]).reshape(1, LANES)
    out["d_skip"] = jnp.repeat(p["d_skip"][0].astype(F32), HEAD_DIM).reshape(1, D_INNER)
    out["d_norm_g"] = p["d_norm_g"][0].astype(F32)
    out["d_w_out"] = p["d_w_out"][0].astype(BF16)
    ffn = []
    for i in range(DEPTH):
        wu = p["ffn_w_up"][i].astype(BF16)
        ffn.append((_chunk_cols(wu[:, :D_FF]), _chunk_cols(wu[:, D_FF:]),
                    _chunk_cols(p["ffn_conv_w"][i].astype(F32)),
                    _chunk_cols(p["ffn_conv_b"][i].astype(F32).reshape(1, D_FF)),
                    p["ffn_w_down"][i].astype(BF16).reshape(D_FF // MID_CHUNK, MID_CHUNK, D_MODEL)))
    out["ffn"] = ffn
    out["a_bias"] = [_attn_bias(p["rel_bias"], window // (2 * r), r, A_PAIR_HEADS)
                     for window, r in DILATIONS]
    out["b_bias"] = _attn_bias(p["rel_bias"], B_HALF_WINDOW, 1, B_PAIR_HEADS)
    out["norm_g"] = p["norm_g"].astype(F32)
    out["final_g"] = p["final_g"].astype(F32)
    return out


def _mixer_dilated(x2, g, w, bsz, seq):
    aw = N_HEADS * HEAD_DIM
    d = x2.shape[1]
    prev = None
    prev_r = 1
    for gi, (window, r) in enumerate(DILATIONS):
        length = seq // r
        if r == 1:
            (qkv,) = _norm_proj(x2, g, w["a_w_qkv"][gi], [(0, 3 * aw)], [BF16])
        else:
            qkv = _norm_proj_class_major(x2, g, w["a_w_qkv"][gi], bsz, seq, r)
        view = qkv.reshape(bsz * r, length, 3 * aw)
        if prev is not None:
            k = r // prev_r

            def regroup(t):
                t = t.reshape(bsz, prev_r, length, k, t.shape[-1]).transpose(0, 3, 1, 2, 4)
                return t.reshape(bsz * r, length, t.shape[-1])

            prev = (regroup(prev[0]), regroup(prev[1]))
        res = _attention(
            view, view, view, qcol=lambda c: 0, kcol=lambda c: 1, vcol=lambda c: 2,
            kv_width=aw, bias=w["a_bias"][gi], length=length, stride=1, halo=window // (2 * r),
            pair_heads=A_PAIR_HEADS, kv_div=1, prev=prev, want_lse=True, out_dtype=BF16)
        prev = tuple(res)
        prev_r = r
    o = prev[0].reshape(bsz, prev_r, seq // prev_r, aw).transpose(0, 2, 1, 3)
    return _out_proj(x2, o.reshape(bsz * seq, aw), w["a_w_o"])


def _mixer_window(x2, g, w, bsz, seq):
    aw = N_HEADS * HEAD_DIM
    kvw = B_KV_HEADS * HEAD_DIM
    n = aw + 2 * kvw
    (qkv,) = _norm_proj(x2, g, w["b_w_qkv"], [(0, n)], [BF16])
    view = qkv.reshape(bsz, seq, n)
    (o,) = _attention(view[:, :, :aw], view, view,
                      qcol=lambda c: 0, kcol=lambda c: aw // kvw, vcol=lambda c: aw // kvw + 1,
                      kv_width=kvw, bias=w["b_bias"], length=seq, stride=1, halo=B_HALF_WINDOW,
                      pair_heads=B_PAIR_HEADS, kv_div=4, sink=w["b_sink"], out_dtype=BF16)
    return _out_proj(x2, o.reshape(bsz * seq, aw), w["b_w_o"])


def _mixer_ssd(x2, g, w, bsz, seq):
    z, xbc, dt_raw = _norm_proj(
        x2, g, w["d_w_in"],
        [(0, D_INNER), (D_INNER, CONV_DIM), (D_INNER + CONV_DIM, LANES)], [BF16, BF16, F32])
    xbc = xbc.reshape(bsz, seq, CONV_DIM)
    dt_raw = dt_raw.reshape(bsz, seq, LANES)
    y, xc = _ssd(xbc, dt_raw, w["d_dt_bias"], w["d_a_neg"], 0,
                 conv=(w["d_conv_w"], w["d_conv_b"], w["d_skip"]))
    y = _ssd(xc, dt_raw, w["d_dt_bias"], w["d_a_neg"], 1, y_in=y)
    return _ssd_out(x2, y.reshape(bsz * seq, D_INNER), z, w["d_norm_g"], w["d_w_out"])


def _trunk(x, w):
    bsz, seq, d = x.shape
    x2 = x.reshape(bsz * seq, d)
    ng = w["norm_g"]
    for i in range(DEPTH):
        kind = i % 4
        if kind == 0:
            x2 = _mixer_dilated(x2, ng[i, 0], w, bsz, seq)
        elif kind == 1:
            x2 = _mixer_window(x2, ng[i, 0], w, bsz, seq)
        elif kind == 2:
            x2 = _gated_mlp(x2, ng[i, 0], w["c_w"], "sconv", seq)
        else:
            x2 = _mixer_ssd(x2, ng[i, 0], w, bsz, seq)
        x2 = _gated_mlp(x2, ng[i, 1], w["ffn"][i], "ffn", seq,
                        final_g=w["final_g"] if i == DEPTH - 1 else None)
    return x2.reshape(bsz, seq, d)


def kernel(x_prompt, x_sample, rel_bias, a_w_qkv, a_w_o, b_w_qkv, b_sink, b_w_o, c_w_in, c_conv_w,
           c_w_out, d_w_in, d_conv_w, d_conv_b, d_dt_bias, d_a_log, d_skip, d_norm_g, d_w_out,
           ffn_w_up, ffn_conv_w, ffn_conv_b, ffn_w_down, norm_g, final_g):
    w = _prepare(dict(
        rel_bias=rel_bias, a_w_qkv=a_w_qkv, a_w_o=a_w_o, b_w_qkv=b_w_qkv, b_sink=b_sink, b_w_o=b_w_o,
        c_w_in=c_w_in, c_conv_w=c_conv_w, c_w_out=c_w_out, d_w_in=d_w_in, d_conv_w=d_conv_w,
        d_conv_b=d_conv_b, d_dt_bias=d_dt_bias, d_a_log=d_a_log, d_skip=d_skip, d_norm_g=d_norm_g,
        d_w_out=d_w_out, ffn_w_up=ffn_w_up, ffn_conv_w=ffn_conv_w, ffn_conv_b=ffn_conv_b,
        ffn_w_down=ffn_w_down, norm_g=norm_g, final_g=final_g))
    return (_trunk(x_prompt, w), _trunk(x_sample, w))
```
